```python
import math
import jax, jax.numpy as jnp
from jax import lax
import numpy as np

D_MODEL = 1024
BATCH = 32
SEQ = 2048
DEPTH = 4

HEAD_DIM = 64
A_HEADS = 4
A_WIDTH = A_HEADS * HEAD_DIM
B_WIDTH = 256
SHORT_CONV = 3
C_WIDTH = 256
CONF_CONV = 31
D_Q_HEADS = 8
D_KV_HEADS = 2
D_GROUP = D_Q_HEADS // D_KV_HEADS
D_WIDTH = D_Q_HEADS * HEAD_DIM
D_KV_WIDTH = D_KV_HEADS * HEAD_DIM
WINDOW = 128
Q_BLOCK = 128
N_BRANCH = 4
REL_BUCKETS = 32
REL_MAX_DIST = 128
PLE_DIM = 256
D_FF = 2816
EPS = 1e-6
NEG_INF = -1e30

A_QKV_END = 3 * A_WIDTH
A_F_END = A_QKV_END + A_HEADS
B_END = A_F_END + 3 * B_WIDTH
C_END = B_END + 2 * C_WIDTH
D_END = C_END + D_WIDTH + 2 * D_KV_WIDTH
N_IN = D_END + N_BRANCH * D_MODEL

kernel_name = 'hybrid_gated_parallel_mixer'


def rms_norm(x, g):
    xf = x.astype(jnp.float32)
    y = xf * lax.rsqrt(jnp.mean(xf * xf, axis=-1, keepdims=True) + EPS)
    return (y * g.astype(jnp.float32)).astype(x.dtype)


def layer_norm(x, g, b):
    xf = x.astype(jnp.float32)
    mu = jnp.mean(xf, axis=-1, keepdims=True)
    xc = xf - mu
    y = xc * lax.rsqrt(jnp.mean(xc * xc, axis=-1, keepdims=True) + EPS)
    return (y * g.astype(jnp.float32) + b.astype(jnp.float32)).astype(x.dtype)


def swiglu_ffn(x, w_gu, w_down):
    gate, up = jnp.split(x @ w_gu, 2, axis=-1)
    return (jax.nn.silu(gate) * up) @ w_down


def causal_depthwise_conv(x, w):
    K, C = w.shape
    return lax.conv_general_dilated(
        x, w[:, None, :].astype(x.dtype), window_strides=(1,), padding=[(K - 1, 0)],
        dimension_numbers=('NWC', 'WIO', 'NWC'), feature_group_count=C)


def t5_causal_bucket(dist):
    max_exact = REL_BUCKETS // 2
    large = max_exact + (jnp.log(jnp.maximum(dist, 1).astype(jnp.float32) / max_exact)
                         / math.log(REL_MAX_DIST / max_exact)
                         * (REL_BUCKETS - max_exact)).astype(jnp.int32)
    large = jnp.minimum(large, REL_BUCKETS - 1)
    return jnp.where(dist < max_exact, dist, large)


def forgetting_attention(q, k, v, log_f):
    S, d = q.shape[1], q.shape[3]
    c = jnp.transpose(jnp.cumsum(log_f, axis=1), (0, 2, 1))
    scale = d ** -0.5
    outs = []
    for blk in range(S // Q_BLOCK):
        q0 = blk * Q_BLOCK
        end = q0 + Q_BLOCK
        s = jnp.einsum('bqhd,bkhd->bhqk', q[:, q0:end], k[:, :end]).astype(jnp.float32) * scale
        decay = c[:, :, q0:end, None] - c[:, :, None, :end]
        causal = (q0 + jnp.arange(Q_BLOCK))[:, None] >= jnp.arange(end)[None, :]
        s = jnp.where(causal, s + decay, NEG_INF)
        pr = jax.nn.softmax(s, axis=-1).astype(v.dtype)
        outs.append(jnp.einsum('bhqk,bkhd->bqhd', pr, v[:, :end]))
    return jnp.concatenate(outs, axis=1)


def sliding_window_attention(q, k, v, sinks, band_bias):
    Bn, S, _, d = q.shape
    nb = S // Q_BLOCK
    qb = q.reshape(Bn, nb, Q_BLOCK, D_KV_HEADS, D_GROUP, d)

    def band(t):
        tb = t.reshape(Bn, nb, Q_BLOCK, D_KV_HEADS, d)
        prev = jnp.concatenate([jnp.zeros_like(tb[:, :1]), tb[:, :-1]], axis=1)
        return jnp.concatenate([prev, tb], axis=2)

    kb, vb = band(k), band(v)
    s = jnp.einsum('bnqkgd,bnskd->bnkgqs', qb, kb).astype(jnp.float32) * d ** -0.5
    s = s + band_bias.reshape(D_KV_HEADS, D_GROUP, Q_BLOCK, 2 * Q_BLOCK)
    kj = jnp.arange(2 * Q_BLOCK)[None, :]
    dist = jnp.arange(Q_BLOCK)[:, None] + Q_BLOCK - kj
    in_window = (dist >= 0) & (dist < WINDOW)
    key_pos = jnp.arange(nb)[:, None] * Q_BLOCK - Q_BLOCK + kj
    valid = in_window[None] & (key_pos >= 0)[:, None, :]
    s = jnp.where(valid[None, :, None, None], s, NEG_INF)
    sink = sinks.astype(jnp.float32).reshape(1, 1, D_KV_HEADS, D_GROUP, 1, 1)
    m = jnp.maximum(jnp.max(s, axis=-1, keepdims=True), sink)
    e = jnp.exp(s - m)
    pr = (e / (jnp.sum(e, axis=-1, keepdims=True) + jnp.exp(sink - m))).astype(v.dtype)
    out = jnp.einsum('bnkgqs,bnskd->bnqkgd', pr, vb)
    return out.reshape(Bn, S, D_Q_HEADS * d)


def setup_inputs(seed: int = 0) -> dict:
    key = jax.random.key(seed)
    ks = iter(jax.random.split(key, 48))

    def nrm(shape, scale):
        return scale * jax.random.normal(next(ks), shape, jnp.float32)

    def gain(shape):
        return 1.0 + nrm(shape, 0.05)

    D, F = D_MODEL, D_FF
    return {
        'x': nrm((BATCH, SEQ, D), 1.0),
        'p': nrm((DEPTH, BATCH, SEQ, PLE_DIM), 1.0),
        'ffn1_norm_pre': gain((DEPTH, D)),
        'ffn1_w_gu': nrm((DEPTH, D, 2 * F), D ** -0.5),
        'ffn1_w_down': nrm((DEPTH, F, D), F ** -0.5),
        'ffn1_norm_post': gain((DEPTH, D)),
        'mix_norm_pre': gain((DEPTH, D)),
        'w_in': nrm((DEPTH, D, N_IN), D ** -0.5),
        'b_forget': 3.0 + nrm((DEPTH, A_HEADS), 1.0),
        'b_gate': nrm((DEPTH, N_BRANCH * D), 0.1),
        'conv_short': nrm((DEPTH, SHORT_CONV, B_WIDTH), SHORT_CONV ** -0.5),
        'conv_dw': nrm((DEPTH, CONF_CONV, C_WIDTH), CONF_CONV ** -0.5),
        'conv_dw_bias': nrm((DEPTH, C_WIDTH), 0.02),
        'conv_ln_gain': gain((DEPTH, C_WIDTH)),
        'conv_ln_bias': nrm((DEPTH, C_WIDTH), 0.02),
        'attn_sinks': nrm((DEPTH, D_Q_HEADS), 0.5),
        'rel_bias': nrm((REL_BUCKETS, D_Q_HEADS), 0.5),
        'w_br_a': nrm((DEPTH, A_WIDTH, D), A_WIDTH ** -0.5),
        'w_br_b': nrm((DEPTH, B_WIDTH, D), B_WIDTH ** -0.5),
        'w_br_c': nrm((DEPTH, C_WIDTH, D), C_WIDTH ** -0.5),
        'w_br_d': nrm((DEPTH, D_WIDTH, D), D_WIDTH ** -0.5),
        'w_o': nrm((DEPTH, D, D), D ** -0.5),
        'mix_norm_post': gain((DEPTH, D)),
        'ffn2_norm_pre': gain((DEPTH, D)),
        'ffn2_w_gu': nrm((DEPTH, D, 2 * F), D ** -0.5),
        'ffn2_w_down': nrm((DEPTH, F, D), F ** -0.5),
        'ffn2_norm_post': gain((DEPTH, D)),
        'ple_norm_gate': gain((DEPTH, D)),
        'w_ple_gate': nrm((DEPTH, D, D), D ** -0.5),
        'w_ple': nrm((DEPTH, PLE_DIM, D), PLE_DIM ** -0.5),
        'ple_norm_post': gain((DEPTH, D)),
    }


def reference(x, p, ffn1_norm_pre, ffn1_w_gu, ffn1_w_down, ffn1_norm_post,
              mix_norm_pre, w_in, b_forget, b_gate, conv_short, conv_dw, conv_dw_bias,
              conv_ln_gain, conv_ln_bias, attn_sinks, rel_bias,
              w_br_a, w_br_b, w_br_c, w_br_d, w_o, mix_norm_post,
              ffn2_norm_pre, ffn2_w_gu, ffn2_w_down, ffn2_norm_post,
              ple_norm_gate, w_ple_gate, w_ple, ple_norm_post):
    Bn, S, _ = x.shape
    band_dist = jnp.maximum(jnp.arange(Q_BLOCK)[:, None] + Q_BLOCK - jnp.arange(2 * Q_BLOCK)[None, :], 0)
    band_bias = jnp.transpose(rel_bias[t5_causal_bucket(band_dist)], (2, 0, 1)).astype(jnp.float32)

    h = x
    for i in range(DEPTH):
        f1 = swiglu_ffn(rms_norm(h, ffn1_norm_pre[i]), ffn1_w_gu[i], ffn1_w_down[i])
        h = h + 0.5 * rms_norm(f1, ffn1_norm_post[i])

        u = rms_norm(h, mix_norm_pre[i])
        proj = u @ w_in[i]
        a_qkv = proj[..., :A_QKV_END]
        a_f = proj[..., A_QKV_END:A_F_END]
        b_in = proj[..., A_F_END:B_END]
        c_in = proj[..., B_END:C_END]
        d_qkv = proj[..., C_END:D_END]
        gates = proj[..., D_END:]

        qa, ka, va = [t.reshape(Bn, S, A_HEADS, HEAD_DIM) for t in jnp.split(a_qkv, 3, axis=-1)]
        log_f = jax.nn.log_sigmoid(a_f.astype(jnp.float32) + b_forget[i].astype(jnp.float32))
        ya = forgetting_attention(qa, ka, va, log_f).reshape(Bn, S, A_WIDTH)

        bg, cg, xb = jnp.split(b_in, 3, axis=-1)
        yb = bg * causal_depthwise_conv(cg * xb, conv_short[i])

        glu = c_in[..., :C_WIDTH] * jax.nn.sigmoid(c_in[..., C_WIDTH:])
        yc = causal_depthwise_conv(glu, conv_dw[i]) + conv_dw_bias[i]
        yc = jax.nn.silu(layer_norm(yc, conv_ln_gain[i], conv_ln_bias[i]))

        qd = d_qkv[..., :D_WIDTH].reshape(Bn, S, D_Q_HEADS, HEAD_DIM)
        kd = d_qkv[..., D_WIDTH:D_WIDTH + D_KV_WIDTH].reshape(Bn, S, D_KV_HEADS, HEAD_DIM)
        vd = d_qkv[..., D_WIDTH + D_KV_WIDTH:].reshape(Bn, S, D_KV_HEADS, HEAD_DIM)
        yd = sliding_window_attention(qd, kd, vd, attn_sinks[i], band_bias)

        g = jax.nn.sigmoid(gates + b_gate[i]).reshape(Bn, S, N_BRANCH, D_MODEL)
        merged = (g[..., 0, :] * (ya @ w_br_a[i]) + g[..., 1, :] * (yb @ w_br_b[i])
                  + g[..., 2, :] * (yc @ w_br_c[i]) + g[..., 3, :] * (yd @ w_br_d[i]))
        h = h + rms_norm(merged @ w_o[i], mix_norm_post[i])

        f2 = swiglu_ffn(rms_norm(h, ffn2_norm_pre[i]), ffn2_w_gu[i], ffn2_w_down[i])
        h = h + 0.5 * rms_norm(f2, ffn2_norm_post[i])

        pg = jax.nn.sigmoid(rms_norm(h, ple_norm_gate[i]) @ w_ple_gate[i])
        h = h + pg * rms_norm(p[i] @ w_ple[i], ple_norm_post[i])
    return h
```

```python
import functools
import math

import jax
import jax.numpy as jnp
from jax import lax
from jax.experimental import pallas as pl
from jax.experimental.pallas import tpu as pltpu

F32 = jnp.float32
BF16 = jnp.bfloat16

HEAD_DIM = 64
A_HEADS = 4
A_WIDTH = A_HEADS * HEAD_DIM
B_WIDTH = 256
SHORT_CONV = 3
C_WIDTH = 256
CONF_CONV = 31
D_Q_HEADS = 8
D_KV_HEADS = 2
D_WIDTH = D_Q_HEADS * HEAD_DIM
D_KV_WIDTH = D_KV_HEADS * HEAD_DIM
WINDOW = 128
Q_BLOCK = 128
N_BRANCH = 4
REL_BUCKETS = 32
REL_MAX_DIST = 128
EPS = 1e-6
NEG_INF = -1e30

LANES = 128
SUBLANES = 8
VMEM_LIMIT = 56 * 1024 * 1024

A_QKV_END = 3 * A_WIDTH
A_F_END = A_QKV_END + A_HEADS
B_END = A_F_END + 3 * B_WIDTH
C_END = B_END + 2 * C_WIDTH
D_END = C_END + D_WIDTH + 2 * D_KV_WIDTH

D_HEAD_PERM = (0, 4, 1, 5, 2, 6, 3, 7)

P_A = 0
P_B = P_A + 3 * A_WIDTH
P_C = P_B + 3 * B_WIDTH
P_D = P_C + 2 * C_WIDTH
P_F = P_D + D_WIDTH + 2 * D_KV_WIDTH
P_END = P_F + LANES


def _rms(x, g):
    return x * lax.rsqrt(jnp.mean(x * x, axis=-1, keepdims=True) + EPS) * g


def _dot(a, b):
    return jnp.dot(a, b, preferred_element_type=F32)


def _dot_nt(a, b):
    return lax.dot_general(a, b, (((1,), (1,)), ((), ())), preferred_element_type=F32)


def _const_spec(shape, index):
    return pl.BlockSpec(shape, lambda *_: index, pipeline_mode=pl.Buffered(1))


def _params(n_axes):
    return pltpu.CompilerParams(dimension_semantics=("arbitrary",) * n_axes,
                                vmem_limit_bytes=VMEM_LIMIT)


def _ffn_body(*refs, d_ff, n_chunks, has_ple):
    if has_ple:
        (h_ref, p_ref, gpre_ref, wgu_ref, wdn_ref, gpost_ref,
         ggate_ref, wpg_ref, wple_ref, gple_ref, o_ref, act_ref) = refs
    else:
        h_ref, gpre_ref, wgu_ref, wdn_ref, gpost_ref, o_ref, act_ref = refs
    x = h_ref[...]
    u = _rms(x, gpre_ref[...]).astype(BF16)
    fc = d_ff // n_chunks
    for c in range(n_chunks):
        gate = _dot(u, wgu_ref[:, c * fc:(c + 1) * fc])
        up = _dot(u, wgu_ref[:, d_ff + c * fc:d_ff + (c + 1) * fc])
        act_ref[:, c * fc:(c + 1) * fc] = (gate * jax.nn.sigmoid(gate) * up).astype(BF16)
    f = _dot(act_ref[...], wdn_ref[...])
    y = x + 0.5 * _rms(f, gpost_ref[...])
    if has_ple:
        pg = jax.nn.sigmoid(_dot(_rms(y, ggate_ref[...]).astype(BF16), wpg_ref[...]))
        pe = _dot(p_ref[...].astype(BF16), wple_ref[...])
        y = y + pg * _rms(pe, gple_ref[...])
    o_ref[...] = y


def _ffn(h, layer, gpre, wgu, wdn, gpost, ple=None, *, tm):
    n, d = h.shape
    d_ff = wdn.shape[1]
    tm = min(tm, n)
    row = lambda t: (t, 0)
    vec = _const_spec((None, 1, d), (layer, 0, 0))
    in_specs = [pl.BlockSpec((tm, d), row)]
    args = [h]
    if ple is not None:
        p, ggate, wpg, wple, gple = ple
        in_specs.append(pl.BlockSpec((None, tm, p.shape[-1]), lambda t: (layer, t, 0)))
        args.append(p)
    in_specs += [vec, _const_spec((None, d, 2 * d_ff), (layer, 0, 0)),
                 _const_spec((None, d_ff, d), (layer, 0, 0)), vec]
    args += [gpre, wgu, wdn, gpost]
    if ple is not None:
        in_specs += [vec, _const_spec((None, d, d), (layer, 0, 0)),
                     _const_spec((None, p.shape[-1], d), (layer, 0, 0)), vec]
        args += [ggate, wpg, wple, gple]
    return pl.pallas_call(
        functools.partial(_ffn_body, d_ff=d_ff, n_chunks=2, has_ple=ple is not None),
        grid=(n // tm,),
        in_specs=in_specs,
        out_specs=pl.BlockSpec((tm, d), row),
        out_shape=jax.ShapeDtypeStruct((n, d), F32),
        scratch_shapes=[pltpu.VMEM((tm, d_ff), BF16)],
        compiler_params=_params(1),
        name="ffn_ple" if ple is not None else "ffn",
    )(*args)


def _inproj_body(h_ref, g_ref, w_ref, bf_ref, aqkv_ref, bz_ref, glu_ref, dqkv_ref, logf_ref):
    u = _rms(h_ref[...], g_ref[...]).astype(BF16)
    scale = HEAD_DIM ** -0.5
    a = _dot(u, w_ref[:, P_A:P_B])
    aqkv_ref[:, :A_WIDTH] = (a[:, :A_WIDTH] * scale).astype(BF16)
    aqkv_ref[:, A_WIDTH:] = a[:, A_WIDTH:].astype(BF16)
    b = _dot(u, w_ref[:, P_B:P_C])
    bz_ref[:, :B_WIDTH] = b[:, :B_WIDTH]
    bz_ref[:, B_WIDTH:] = b[:, B_WIDTH:2 * B_WIDTH] * b[:, 2 * B_WIDTH:]
    c = _dot(u, w_ref[:, P_C:P_D])
    glu_ref[...] = c[:, :C_WIDTH] * jax.nn.sigmoid(c[:, C_WIDTH:])
    dd = _dot(u, w_ref[:, P_D:P_F])
    dqkv_ref[:, :D_WIDTH] = (dd[:, :D_WIDTH] * scale).astype(BF16)
    dqkv_ref[:, D_WIDTH:] = dd[:, D_WIDTH:].astype(BF16)
    af = _dot(u, w_ref[:, P_F:P_END]) + bf_ref[...]
    t = af.T[:SUBLANES, :]
    logf_ref[...] = jnp.minimum(t, 0.0) - jnp.log1p(jnp.exp(-jnp.abs(t)))


def _inproj(h, layer, g, w, bf, *, batch, seq, tm):
    n, d = h.shape
    tm = min(tm, seq)
    nt = seq // tm
    row = lambda b, t: (b * nt + t, 0)
    out_shape = (jax.ShapeDtypeStruct((n, 3 * A_WIDTH), BF16),
                 jax.ShapeDtypeStruct((n, 2 * B_WIDTH), F32),
                 jax.ShapeDtypeStruct((n, C_WIDTH), F32),
                 jax.ShapeDtypeStruct((n, D_WIDTH + 2 * D_KV_WIDTH), BF16),
                 jax.ShapeDtypeStruct((batch, SUBLANES, seq), F32))
    out_specs = (pl.BlockSpec((tm, 3 * A_WIDTH), row),
                 pl.BlockSpec((tm, 2 * B_WIDTH), row),
                 pl.BlockSpec((tm, C_WIDTH), row),
                 pl.BlockSpec((tm, D_WIDTH + 2 * D_KV_WIDTH), row),
                 pl.BlockSpec((None, SUBLANES, tm), lambda b, t: (b, 0, t)))
    return pl.pallas_call(
        _inproj_body,
        grid=(batch, nt),
        in_specs=[pl.BlockSpec((tm, d), row),
                  _const_spec((None, 1, d), (layer, 0, 0)),
                  _const_spec((None, d, P_END), (layer, 0, 0)),
                  _const_spec((None, 1, LANES), (layer, 0, 0))],
        out_specs=out_specs,
        out_shape=out_shape,
        compiler_params=_params(2),
        name="inproj",
    )(h, g, w, bf)


def _split3(x):
    hi = x.astype(BF16)
    r = x - hi.astype(F32)
    mid = r.astype(BF16)
    lo = (r - mid.astype(F32)).astype(BF16)
    return hi, mid, lo


def _cumsum_lanes(x):
    rows, s = x.shape
    nb = s // LANES
    stacked = jnp.concatenate([x[:, b * LANES:(b + 1) * LANES] for b in range(nb)], axis=0)
    r = lax.broadcasted_iota(jnp.int32, (LANES, LANES), 0)
    c = lax.broadcasted_iota(jnp.int32, (LANES, LANES), 1)
    upper = jnp.where(r <= c, 1.0, 0.0).astype(BF16)
    hi, mid, lo = _split3(stacked)
    within = (_dot(lo, upper) + _dot(mid, upper)) + _dot(hi, upper)
    tot = jnp.broadcast_to(within[:, LANES - 1:], within.shape)
    n = rows * nb
    rr = lax.broadcasted_iota(jnp.int32, (n, n), 0)
    cc = lax.broadcasted_iota(jnp.int32, (n, n), 1)
    shift = rows.bit_length() - 1
    same_row = (rr & (rows - 1)) == (cc & (rows - 1))
    earlier = lax.shift_right_logical(cc, shift) < lax.shift_right_logical(rr, shift)
    prev = jnp.where(same_row & earlier, 1.0, 0.0).astype(BF16)
    hi, mid, lo = _split3(tot)
    total = within + ((_dot(prev, lo) + _dot(prev, mid)) + _dot(prev, hi))
    return [total[b * rows:(b + 1) * rows, :] for b in range(nb)]


def _fox_body(q_ref, k_ref, v_ref, logf_ref, o_ref, c_ref, *, tq):
    pair = pl.program_id(1)
    qi = pl.program_id(2)
    per_blk = tq // LANES

    @pl.when(qi == 0)
    def _():
        blocks = _cumsum_lanes(logf_ref[...])
        for b, blk in enumerate(blocks):
            lanes = slice((b % per_blk) * LANES, (b % per_blk + 1) * LANES)
            c_ref[b // per_blk, :, lanes] = blk
            c_ref[b // per_blk, 0:2, lanes] = jnp.where(pair == 0, blk[0:2, :], blk[2:4, :])

    q = q_ref[...]
    lane = lax.broadcasted_iota(jnp.int32, (1, LANES), 1)
    is_lo = lane < HEAD_DIM
    zero = jnp.zeros_like(q)
    q_heads = (jnp.where(is_lo, q, zero), jnp.where(is_lo, zero, q))
    c_here = c_ref[qi]
    c_end = (c_here[0:1, tq - 1:tq], c_here[1:2, tq - 1:tq])
    row_id = lax.broadcasted_iota(jnp.int32, (tq, tq), 0)
    col_id = lax.broadcasted_iota(jnp.int32, (tq, tq), 1)

    def step(kj, carry, masked):
        start = pl.multiple_of(kj * tq, tq)
        k = k_ref[pl.ds(start, tq), :]
        v = v_ref[pl.ds(start, tq), :]
        cb = c_ref[kj]
        out = []
        for hh in range(2):
            m, l, acc = carry[hh]
            s = _dot_nt(q_heads[hh], k) + (c_end[hh] - cb[hh:hh + 1, :])
            if masked:
                s = jnp.where(row_id >= col_id, s, NEG_INF)
            m_new = jnp.maximum(m, jnp.max(s, axis=-1, keepdims=True))
            p = jnp.exp(s - m_new)
            alpha = jnp.exp(m - m_new)
            l_new = alpha * l + jnp.sum(p, axis=-1, keepdims=True)
            acc_new = alpha * acc + _dot(p.astype(BF16), v)
            out.append((m_new, l_new, acc_new))
        return tuple(out)

    init_one = (jnp.full((tq, 1), NEG_INF, F32), jnp.zeros((tq, 1), F32),
                jnp.zeros((tq, LANES), F32))
    carry = lax.fori_loop(0, qi, lambda kj, cr: step(kj, cr, False), (init_one, init_one))
    (_, l0, a0), (_, l1, a1) = step(qi, carry, True)
    o_ref[...] = jnp.where(is_lo, a0 * (1.0 / l0), a1 * (1.0 / l1)).astype(o_ref.dtype)


def _fox_attention(aqkv, logf, *, batch, seq, tq):
    tq = min(tq, seq)
    nq = seq // tq
    n_pairs = A_WIDTH // LANES
    qkv = aqkv.reshape(batch, seq, 3 * A_WIDTH)
    return pl.pallas_call(
        functools.partial(_fox_body, tq=tq),
        grid=(batch, n_pairs, nq),
        in_specs=[pl.BlockSpec((None, tq, LANES), lambda b, p, i: (b, i, p)),
                  pl.BlockSpec((None, seq, LANES), lambda b, p, i: (b, 0, n_pairs + p)),
                  pl.BlockSpec((None, seq, LANES), lambda b, p, i: (b, 0, 2 * n_pairs + p)),
                  pl.BlockSpec((None, SUBLANES, seq), lambda b, p, i: (b, 0, 0))],
        out_specs=pl.BlockSpec((None, tq, LANES), lambda b, p, i: (b, i, p)),
        out_shape=jax.ShapeDtypeStruct((batch, seq, A_WIDTH), BF16),
        scratch_shapes=[pltpu.VMEM((nq, SUBLANES, tq), F32)],
        compiler_params=_params(3),
        name="fox_attention",
    )(qkv, qkv, qkv, logf)


def _t5_causal_bucket(dist):
    max_exact = REL_BUCKETS // 2
    large = max_exact + (jnp.log(jnp.maximum(dist, 1).astype(F32) / max_exact)
                         / math.log(REL_MAX_DIST / max_exact)
                         * (REL_BUCKETS - max_exact)).astype(jnp.int32)
    large = jnp.minimum(large, REL_BUCKETS - 1)
    return jnp.where(dist < max_exact, dist, large)


def _band_bias_body(bucket_ref, rel_ref, o_ref):
    bucket = bucket_ref[...]
    for h in range(D_Q_HEADS):
        acc = jnp.zeros(bucket.shape, F32)
        for b in range(REL_BUCKETS):
            acc = jnp.where(bucket == b, rel_ref[b, h], acc)
        o_ref[h] = acc


def _band_bias(rel_bias):
    dist = jnp.maximum(jnp.arange(Q_BLOCK)[:, None] + Q_BLOCK - jnp.arange(2 * Q_BLOCK)[None, :], 0)
    bucket = _t5_causal_bucket(dist).astype(jnp.int32)
    return pl.pallas_call(
        _band_bias_body,
        in_specs=[pl.BlockSpec(memory_space=pltpu.VMEM), pl.BlockSpec(memory_space=pltpu.SMEM)],
        out_specs=pl.BlockSpec(memory_space=pltpu.VMEM),
        out_shape=jax.ShapeDtypeStruct((D_Q_HEADS, Q_BLOCK, 2 * Q_BLOCK), F32),
        name="band_bias",
    )(bucket, rel_bias)


def _swa_body(q_ref, kp_ref, kc_ref, vp_ref, vc_ref, bias_ref, sink_ref, o_ref):
    n = pl.program_id(1)
    lane = lax.broadcasted_iota(jnp.int32, (1, LANES), 1)
    is_lo = lane < HEAD_DIM
    row_id = lax.broadcasted_iota(jnp.int32, (Q_BLOCK, Q_BLOCK), 0)
    col_id = lax.broadcasted_iota(jnp.int32, (Q_BLOCK, Q_BLOCK), 1)
    valid_prev = col_id > row_id + jnp.where(n > 0, 0, Q_BLOCK)
    valid_cur = col_id <= row_id
    kp, kc, vp, vc = kp_ref[...], kc_ref[...], vp_ref[...], vc_ref[...]
    for j in range(D_Q_HEADS // 2):
        q = q_ref[:, j * LANES:(j + 1) * LANES]
        zero = jnp.zeros_like(q)
        halves = []
        for half in range(2):
            head = j + half * (D_Q_HEADS // 2)
            qm = jnp.where(is_lo, q, zero) if half == 0 else jnp.where(is_lo, zero, q)
            bias = bias_ref[head]
            sp = jnp.where(valid_prev, _dot_nt(qm, kp) + bias[:, :Q_BLOCK], NEG_INF)
            sc = jnp.where(valid_cur, _dot_nt(qm, kc) + bias[:, Q_BLOCK:], NEG_INF)
            sink = sink_ref[head:head + 1, 0:1]
            m = jnp.maximum(jnp.maximum(jnp.max(sp, axis=-1, keepdims=True),
                                        jnp.max(sc, axis=-1, keepdims=True)), sink)
            ep = jnp.exp(sp - m)
            ec = jnp.exp(sc - m)
            denom = (jnp.sum(ep, axis=-1, keepdims=True) + jnp.sum(ec, axis=-1, keepdims=True)
                     + jnp.exp(sink - m))
            inv = 1.0 / denom
            halves.append(_dot((ep * inv).astype(BF16), vp) + _dot((ec * inv).astype(BF16), vc))
        o_ref[:, j * LANES:(j + 1) * LANES] = jnp.where(is_lo, halves[0], halves[1]).astype(o_ref.dtype)


def _swa_attention(dqkv, band_bias, sinks, layer, *, batch, seq):
    nb = seq // Q_BLOCK
    qkv = dqkv.reshape(batch, seq, D_WIDTH + 2 * D_KV_WIDTH)
    k_blk = D_WIDTH // LANES
    v_blk = k_blk + 1
    prev = lambda b, n: jnp.maximum(n - 1, 0)
    return pl.pallas_call(
        _swa_body,
        grid=(batch, nb),
        in_specs=[pl.BlockSpec((None, Q_BLOCK, D_WIDTH), lambda b, n: (b, n, 0)),
                  pl.BlockSpec((None, Q_BLOCK, LANES), lambda b, n: (b, prev(b, n), k_blk)),
                  pl.BlockSpec((None, Q_BLOCK, LANES), lambda b, n: (b, n, k_blk)),
                  pl.BlockSpec((None, Q_BLOCK, LANES), lambda b, n: (b, prev(b, n), v_blk)),
                  pl.BlockSpec((None, Q_BLOCK, LANES), lambda b, n: (b, n, v_blk)),
                  _const_spec((D_Q_HEADS, Q_BLOCK, 2 * Q_BLOCK), (0, 0, 0)),
                  _const_spec((None, D_Q_HEADS, LANES), (layer, 0, 0))],
        out_specs=pl.BlockSpec((None, Q_BLOCK, D_WIDTH), lambda b, n: (b, n, 0)),
        out_shape=jax.ShapeDtypeStruct((batch, seq, D_WIDTH), BF16),
        compiler_params=_params(2),
        name="swa_attention",
    )(qkv, qkv, qkv, qkv, qkv, band_bias, sinks)


CONV_HALO = 32
SHORT_HALO = SUBLANES
CONV_ROWS = 64


def _merge_body(h_ref, ya_ref, yd_ref, bz_ref, bzh_ref, glu_ref, gluh_ref,
                gpre_ref, wg_ref, bg_ref, wa_ref, wb_ref, wc_ref, wd_ref, wo_ref, gpost_ref,
                cs_ref, cdw_ref, cdb_ref, lng_ref, lnb_ref,
                o_ref, zext_ref, gext_ref, yc_ref, *, tm):
    first = pl.program_id(1) == 0
    d = h_ref.shape[-1]

    zext_ref[:SHORT_HALO, :] = jnp.where(first, 0.0, bzh_ref[:, B_WIDTH:])
    zext_ref[SHORT_HALO:, :] = bz_ref[:, B_WIDTH:]
    conv = jnp.zeros((tm, B_WIDTH), F32)
    for k in range(SHORT_CONV):
        off = SHORT_HALO - (SHORT_CONV - 1) + k
        conv = conv + cs_ref[k:k + 1, :] * zext_ref[off:off + tm, :]
    yb = bz_ref[:, :B_WIDTH] * conv

    gext_ref[:CONV_HALO, :] = jnp.where(first, 0.0, gluh_ref[...])
    gext_ref[CONV_HALO:, :] = glu_ref[...]
    for r0 in range(0, tm, CONV_ROWS):
        acc = jnp.broadcast_to(cdb_ref[...], (CONV_ROWS, C_WIDTH))
        for k in range(CONF_CONV):
            off = r0 + CONV_HALO - (CONF_CONV - 1) + k
            acc = acc + cdw_ref[k:k + 1, :] * gext_ref[off:off + CONV_ROWS, :]
        mu = jnp.mean(acc, axis=-1, keepdims=True)
        xc = acc - mu
        y = xc * lax.rsqrt(jnp.mean(xc * xc, axis=-1, keepdims=True) + EPS)
        y = y * lng_ref[...] + lnb_ref[...]
        yc_ref[r0:r0 + CONV_ROWS, :] = (y * jax.nn.sigmoid(y)).astype(BF16)

    x = h_ref[...]
    u = _rms(x, gpre_ref[...]).astype(BF16)
    branches = (_dot(ya_ref[...], wa_ref[...]), _dot(yb.astype(BF16), wb_ref[...]),
                _dot(yc_ref[...], wc_ref[...]), _dot(yd_ref[...], wd_ref[...]))
    merged = jnp.zeros((tm, d), F32)
    for i, br in enumerate(branches):
        gate = jax.nn.sigmoid(_dot(u, wg_ref[:, i * d:(i + 1) * d]) + bg_ref[:, i * d:(i + 1) * d])
        merged = merged + gate * br
    out = _dot(merged.astype(BF16), wo_ref[...])
    o_ref[...] = x + _rms(out, gpost_ref[...])


def _merge(h, ya, yd, bz, glu, layer, gpre, wg, bg, wa, wb, wc, wd, wo, gpost,
           cs, cdw, cdb, lng, lnb, *, batch, seq, tm):
    n, d = h.shape
    tm = min(tm, seq)
    nt = seq // tm
    row = lambda b, t: (b * nt + t, 0)

    def halo(rows):
        per = tm // rows
        return lambda b, t: (jnp.maximum((b * nt + t) * per - 1, 0), 0)

    lay = lambda *shape: _const_spec((None,) + shape, (layer,) + (0,) * len(shape))
    return pl.pallas_call(
        functools.partial(_merge_body, tm=tm),
        grid=(batch, nt),
        in_specs=[pl.BlockSpec((tm, d), row),
                  pl.BlockSpec((tm, A_WIDTH), row),
                  pl.BlockSpec((tm, D_WIDTH), row),
                  pl.BlockSpec((tm, 2 * B_WIDTH), row),
                  pl.BlockSpec((SHORT_HALO, 2 * B_WIDTH), halo(SHORT_HALO)),
                  pl.BlockSpec((tm, C_WIDTH), row),
                  pl.BlockSpec((CONV_HALO, C_WIDTH), halo(CONV_HALO)),
                  lay(1, d), lay(d, N_BRANCH * d), lay(1, N_BRANCH * d),
                  lay(A_WIDTH, d), lay(B_WIDTH, d), lay(C_WIDTH, d), lay(D_WIDTH, d),
                  lay(d, d), lay(1, d),
                  lay(SHORT_CONV, B_WIDTH), lay(CONF_CONV, C_WIDTH), lay(1, C_WIDTH),
                  lay(1, C_WIDTH), lay(1, C_WIDTH)],
        out_specs=pl.BlockSpec((tm, d), row),
        out_shape=jax.ShapeDtypeStruct((n, d), F32),
        scratch_shapes=[pltpu.VMEM((tm + SHORT_HALO, B_WIDTH), F32),
                        pltpu.VMEM((tm + CONV_HALO, C_WIDTH), F32),
                        pltpu.VMEM((tm, C_WIDTH), BF16)],
        compiler_params=_params(2),
        name="merge",
    )(h, ya.reshape(n, A_WIDTH), yd.reshape(n, D_WIDTH), bz, bz, glu, glu,
      gpre, wg, bg, wa, wb, wc, wd, wo, gpost, cs, cdw, cdb, lng, lnb)


def kernel(x, p, ffn1_norm_pre, ffn1_w_gu, ffn1_w_down, ffn1_norm_post, mix_norm_pre, w_in, b_forget, b_gate, conv_short, conv_dw, conv_dw_bias, conv_ln_gain, conv_ln_bias, attn_sinks, rel_bias, w_br_a, w_br_b, w_br_c, w_br_d, w_o, mix_norm_post, ffn2_norm_pre, ffn2_w_gu, ffn2_w_down, ffn2_norm_post, ple_norm_gate, w_ple_gate, w_ple, ple_norm_post):
    batch, seq, d = x.shape
    depth = w_in.shape[0]
    n = batch * seq
    bf = lambda w: w.astype(BF16)
    vec = lambda g: g.astype(F32)[:, None, :]

    dq = w_in[..., C_END:C_END + D_WIDTH].reshape(depth, d, D_Q_HEADS, HEAD_DIM)
    dq = dq[:, :, D_HEAD_PERM, :].reshape(depth, d, D_WIDTH)
    w_pack = bf(jnp.concatenate(
        [w_in[..., :A_QKV_END], w_in[..., A_F_END:C_END], dq, w_in[..., C_END + D_WIDTH:D_END],
         jnp.pad(w_in[..., A_QKV_END:A_F_END], ((0, 0), (0, 0), (0, LANES - A_HEADS)))], axis=-1))
    w_gate = bf(w_in[..., D_END:])
    b_forget_row = jnp.pad(b_forget.astype(F32), ((0, 0), (0, LANES - A_HEADS)))[:, None, :]
    w_br_d_perm = bf(w_br_d.reshape(depth, D_Q_HEADS, HEAD_DIM, d)[:, D_HEAD_PERM].reshape(depth, D_WIDTH, d))
    sinks = jnp.broadcast_to(attn_sinks.astype(F32)[:, :, None], (depth, D_Q_HEADS, LANES))
    band_bias = _band_bias(rel_bias.astype(F32))

    ffn1_w_gu, ffn1_w_down = bf(ffn1_w_gu), bf(ffn1_w_down)
    ffn2_w_gu, ffn2_w_down = bf(ffn2_w_gu), bf(ffn2_w_down)
    w_br_a, w_br_b, w_br_c, w_o = bf(w_br_a), bf(w_br_b), bf(w_br_c), bf(w_o)
    w_ple_gate, w_ple = bf(w_ple_gate), bf(w_ple)
    p2 = p.reshape(depth, n, p.shape[-1])

    h = x.reshape(n, d)
    for i in range(depth):
        h = _ffn(h, i, vec(ffn1_norm_pre), ffn1_w_gu, ffn1_w_down, vec(ffn1_norm_post), tm=512)
        aqkv, bz, glu, dqkv, logf = _inproj(h, i, vec(mix_norm_pre), w_pack, b_forget_row,
                                            batch=batch, seq=seq, tm=512)
        ya = _fox_attention(aqkv, logf, batch=batch, seq=seq, tq=512)
        yd = _swa_attention(dqkv, band_bias, sinks, i, batch=batch, seq=seq)
        h = _merge(h, ya, yd, bz, glu, i, vec(mix_norm_pre), w_gate, vec(b_gate),
                   w_br_a, w_br_b, w_br_c, w_br_d_perm, w_o, vec(mix_norm_post),
                   conv_short.astype(F32), conv_dw.astype(F32), vec(conv_dw_bias),
                   vec(conv_ln_gain), vec(conv_ln_bias), batch=batch, seq=seq, tm=512)
        h = _ffn(h, i, vec(ffn2_norm_pre), ffn2_w_gu, ffn2_w_down, vec(ffn2_norm_post),
                 ple=(p2, vec(ple_norm_gate), w_ple_gate, w_ple, vec(ple_norm_post)), tm=512)
    return h.reshape(batch, seq, d)
```

```python
import functools
import math

import jax
import jax.numpy as jnp
from jax import lax
from jax.experimental import pallas as pl
from jax.experimental.pallas import tpu as pltpu

F32 = jnp.float32
BF16 = jnp.bfloat16

HEAD_DIM = 64
A_HEADS = 4
A_WIDTH = A_HEADS * HEAD_DIM
B_WIDTH = 256
SHORT_CONV = 3
C_WIDTH = 256
CONF_CONV = 31
D_Q_HEADS = 8
D_KV_HEADS = 2
D_WIDTH = D_Q_HEADS * HEAD_DIM
D_KV_WIDTH = D_KV_HEADS * HEAD_DIM
WINDOW = 128
Q_BLOCK = 128
N_BRANCH = 4
REL_BUCKETS = 32
REL_MAX_DIST = 128
EPS = 1e-6
NEG_INF = -1e30

LANES = 128
SUBLANES = 8
VMEM_LIMIT = 56 * 1024 * 1024
MXU_WIDTH = 256
FFN_TM = 1024
FFN_CHUNK = MXU_WIDTH

A_QKV_END = 3 * A_WIDTH
A_F_END = A_QKV_END + A_HEADS
B_END = A_F_END + 3 * B_WIDTH
C_END = B_END + 2 * C_WIDTH
D_END = C_END + D_WIDTH + 2 * D_KV_WIDTH

D_HEAD_PERM = (0, 4, 1, 5, 2, 6, 3, 7)

P_A = 0
P_B = P_A + 3 * A_WIDTH
P_C = P_B + 3 * B_WIDTH
P_D = P_C + 2 * C_WIDTH
P_F = P_D + D_WIDTH + 2 * D_KV_WIDTH
P_END = P_F + LANES


def _rms(x, g):
    return x * lax.rsqrt(jnp.mean(x * x, axis=-1, keepdims=True) + EPS) * g


def _dot(a, b):
    return jnp.dot(a, b, preferred_element_type=F32)


def _dot_nt(a, b):
    return lax.dot_general(a, b, (((1,), (1,)), ((), ())), preferred_element_type=F32)


def _const_spec(shape, index):
    return pl.BlockSpec(shape, lambda *_: index, pipeline_mode=pl.Buffered(1))


def _params(n_axes):
    return pltpu.CompilerParams(dimension_semantics=("arbitrary",) * n_axes,
                                vmem_limit_bytes=VMEM_LIMIT)


def _ffn_body(*refs, d_ff, n_chunks, has_ple):
    if has_ple:
        (h_ref, p_ref, gpre_ref, wgu_ref, wdn_ref, gpost_ref,
         ggate_ref, wpg_ref, wple_ref, gple_ref, o_ref, act_ref) = refs
    else:
        h_ref, gpre_ref, wgu_ref, wdn_ref, gpost_ref, o_ref, act_ref = refs
    x = h_ref[...]
    u = _rms(x, gpre_ref[...]).astype(BF16)
    fc = d_ff // n_chunks
    for c in range(n_chunks):
        gate = _dot(u, wgu_ref[:, c * fc:(c + 1) * fc])
        up = _dot(u, wgu_ref[:, d_ff + c * fc:d_ff + (c + 1) * fc])
        act_ref[:, c * fc:(c + 1) * fc] = (gate * jax.nn.sigmoid(gate) * up).astype(BF16)
    f = _dot(act_ref[...], wdn_ref[...])
    y = x + 0.5 * _rms(f, gpost_ref[...])
    if has_ple:
        pg = jax.nn.sigmoid(_dot(_rms(y, ggate_ref[...]).astype(BF16), wpg_ref[...]))
        pe = _dot(p_ref[...].astype(BF16), wple_ref[...])
        y = y + pg * _rms(pe, gple_ref[...])
    o_ref[...] = y


def _ffn(h, layer, gpre, wgu, wdn, gpost, ple=None, *, tm):
    n, d = h.shape
    d_ff = wdn.shape[1]
    tm = min(tm, n)
    row = lambda t: (t, 0)
    vec = _const_spec((None, 1, d), (layer, 0, 0))
    in_specs = [pl.BlockSpec((tm, d), row)]
    args = [h]
    if ple is not None:
        p, ggate, wpg, wple, gple = ple
        in_specs.append(pl.BlockSpec((None, tm, p.shape[-1]), lambda t: (layer, t, 0)))
        args.append(p)
    in_specs += [vec, _const_spec((None, d, 2 * d_ff), (layer, 0, 0)),
                 _const_spec((None, d_ff, d), (layer, 0, 0)), vec]
    args += [gpre, wgu, wdn, gpost]
    if ple is not None:
        in_specs += [vec, _const_spec((None, d, d), (layer, 0, 0)),
                     _const_spec((None, p.shape[-1], d), (layer, 0, 0)), vec]
        args += [ggate, wpg, wple, gple]
    return pl.pallas_call(
        functools.partial(_ffn_body, d_ff=d_ff, n_chunks=d_ff // FFN_CHUNK, has_ple=ple is not None),
        grid=(n // tm,),
        in_specs=in_specs,
        out_specs=pl.BlockSpec((tm, d), row),
        out_shape=jax.ShapeDtypeStruct((n, d), F32),
        scratch_shapes=[pltpu.VMEM((tm, d_ff), BF16)],
        compiler_params=_params(1),
        name="ffn_ple" if ple is not None else "ffn",
    )(*args)


CONV_HALO = 32
SHORT_HALO = SUBLANES
CONV_ROWS = 64


def _inproj_body(h_ref, g_ref, w_ref, bf_ref, cs_ref, cdw_ref, cdb_ref, lng_ref, lnb_ref,
                 aqkv_ref, yb_ref, yc_ref, dqkv_ref, logf_ref, zext_ref, gext_ref, gsh_ref):
    tm = h_ref.shape[0]

    @pl.when(pl.program_id(1) == 0)
    def _():
        zext_ref[:SHORT_HALO, :] = jnp.zeros((SHORT_HALO, B_WIDTH), F32)
        gext_ref[:CONV_HALO, :] = jnp.zeros((CONV_HALO, C_WIDTH), F32)

    u = _rms(h_ref[...], g_ref[...]).astype(BF16)
    scale = HEAD_DIM ** -0.5
    a = _dot(u, w_ref[:, P_A:P_B])
    aqkv_ref[:, :A_WIDTH] = (a[:, :A_WIDTH] * scale).astype(BF16)
    aqkv_ref[:, A_WIDTH:] = a[:, A_WIDTH:].astype(BF16)

    b = _dot(u, w_ref[:, P_B:P_C])
    zext_ref[SHORT_HALO:, :] = b[:, B_WIDTH:2 * B_WIDTH] * b[:, 2 * B_WIDTH:]
    conv = jnp.zeros((tm, B_WIDTH), F32)
    for k in range(SHORT_CONV):
        off = SHORT_HALO - (SHORT_CONV - 1) + k
        conv = conv + cs_ref[k:k + 1, :] * zext_ref[off:off + tm, :]
    yb_ref[...] = (b[:, :B_WIDTH] * conv).astype(BF16)
    zext_ref[:SHORT_HALO, :] = zext_ref[tm:tm + SHORT_HALO, :]

    c = _dot(u, w_ref[:, P_C:P_D])
    gext_ref[CONV_HALO:, :] = c[:, :C_WIDTH] * jax.nn.sigmoid(c[:, C_WIDTH:])
    sh_rows = tm + CONV_HALO - SUBLANES
    for s in range(1, SUBLANES):
        gsh_ref[s - 1] = gext_ref[s:s + sh_rows, :]
    for r0 in range(0, tm, CONV_ROWS):
        acc = jnp.broadcast_to(cdb_ref[...], (CONV_ROWS, C_WIDTH))
        for k in range(CONF_CONV):
            off = CONV_HALO - (CONF_CONV - 1) + k
            s = off % SUBLANES
            lo = r0 + off - s
            win = gext_ref[lo:lo + CONV_ROWS, :] if s == 0 else gsh_ref[s - 1, lo:lo + CONV_ROWS, :]
            acc = acc + cdw_ref[k:k + 1, :] * win
        mu = jnp.mean(acc, axis=-1, keepdims=True)
        xc = acc - mu
        y = xc * lax.rsqrt(jnp.mean(xc * xc, axis=-1, keepdims=True) + EPS)
        y = y * lng_ref[...] + lnb_ref[...]
        yc_ref[r0:r0 + CONV_ROWS, :] = (y * jax.nn.sigmoid(y)).astype(BF16)
    gext_ref[:CONV_HALO, :] = gext_ref[tm:tm + CONV_HALO, :]

    dd = _dot(u, w_ref[:, P_D:P_F])
    dqkv_ref[:, :D_WIDTH] = (dd[:, :D_WIDTH] * scale).astype(BF16)
    dqkv_ref[:, D_WIDTH:] = dd[:, D_WIDTH:].astype(BF16)
    af = _dot(u, w_ref[:, P_F:P_END]) + bf_ref[...]
    t = af.T[:SUBLANES, :]
    logf_ref[...] = jnp.minimum(t, 0.0) - jnp.log1p(jnp.exp(-jnp.abs(t)))


def _inproj(h, layer, g, w, bf, cs, cdw, cdb, lng, lnb, *, batch, seq, tm):
    n, d = h.shape
    tm = min(tm, seq)
    nt = seq // tm
    row = lambda b, t: (b * nt + t, 0)
    lay = lambda *shape: _const_spec((None,) + shape, (layer,) + (0,) * len(shape))
    out_shape = (jax.ShapeDtypeStruct((n, 3 * A_WIDTH), BF16),
                 jax.ShapeDtypeStruct((n, B_WIDTH), BF16),
                 jax.ShapeDtypeStruct((n, C_WIDTH), BF16),
                 jax.ShapeDtypeStruct((n, D_WIDTH + 2 * D_KV_WIDTH), BF16),
                 jax.ShapeDtypeStruct((batch, SUBLANES, seq), F32))
    out_specs = (pl.BlockSpec((tm, 3 * A_WIDTH), row),
                 pl.BlockSpec((tm, B_WIDTH), row),
                 pl.BlockSpec((tm, C_WIDTH), row),
                 pl.BlockSpec((tm, D_WIDTH + 2 * D_KV_WIDTH), row),
                 pl.BlockSpec((None, SUBLANES, tm), lambda b, t: (b, 0, t)))
    return pl.pallas_call(
        _inproj_body,
        grid=(batch, nt),
        in_specs=[pl.BlockSpec((tm, d), row), lay(1, d), lay(d, P_END), lay(1, LANES),
                  lay(SHORT_CONV, B_WIDTH), lay(CONF_CONV, C_WIDTH), lay(1, C_WIDTH),
                  lay(1, C_WIDTH), lay(1, C_WIDTH)],
        out_specs=out_specs,
        out_shape=out_shape,
        scratch_shapes=[pltpu.VMEM((tm + SHORT_HALO, B_WIDTH), F32),
                        pltpu.VMEM((tm + CONV_HALO, C_WIDTH), F32),
                        pltpu.VMEM((SUBLANES - 1, tm + CONV_HALO - SUBLANES, C_WIDTH), F32)],
        compiler_params=_params(2),
        name="inproj",
    )(h, g, w, bf, cs, cdw, cdb, lng, lnb)


def _split3(x):
    hi = x.astype(BF16)
    r = x - hi.astype(F32)
    mid = r.astype(BF16)
    lo = (r - mid.astype(F32)).astype(BF16)
    return hi, mid, lo


def _cumsum_lanes(x):
    rows, s = x.shape
    nb = s // LANES
    stacked = jnp.concatenate([x[:, b * LANES:(b + 1) * LANES] for b in range(nb)], axis=0)
    r = lax.broadcasted_iota(jnp.int32, (LANES, LANES), 0)
    c = lax.broadcasted_iota(jnp.int32, (LANES, LANES), 1)
    upper = jnp.where(r <= c, 1.0, 0.0).astype(BF16)
    hi, mid, lo = _split3(stacked)
    within = (_dot(lo, upper) + _dot(mid, upper)) + _dot(hi, upper)
    tot = jnp.broadcast_to(within[:, LANES - 1:], within.shape)
    n = rows * nb
    rr = lax.broadcasted_iota(jnp.int32, (n, n), 0)
    cc = lax.broadcasted_iota(jnp.int32, (n, n), 1)
    shift = rows.bit_length() - 1
    same_row = (rr & (rows - 1)) == (cc & (rows - 1))
    earlier = lax.shift_right_logical(cc, shift) < lax.shift_right_logical(rr, shift)
    prev = jnp.where(same_row & earlier, 1.0, 0.0).astype(BF16)
    hi, mid, lo = _split3(tot)
    total = within + ((_dot(prev, lo) + _dot(prev, mid)) + _dot(prev, hi))
    return [total[b * rows:(b + 1) * rows, :] for b in range(nb)]


def _fox_body(q_ref, k_ref, v_ref, logf_ref, o_ref, c_ref, *, tq):
    pair = pl.program_id(1)
    qi = pl.program_id(2)
    per_blk = tq // LANES

    @pl.when(qi == 0)
    def _():
        blocks = _cumsum_lanes(logf_ref[...])
        for b, blk in enumerate(blocks):
            lanes = slice((b % per_blk) * LANES, (b % per_blk + 1) * LANES)
            c_ref[b // per_blk, :, lanes] = blk
            c_ref[b // per_blk, 0:2, lanes] = jnp.where(pair == 0, blk[0:2, :], blk[2:4, :])

    q = q_ref[...]
    lane = lax.broadcasted_iota(jnp.int32, (1, LANES), 1)
    is_lo = lane < HEAD_DIM
    zero = jnp.zeros_like(q)
    q_heads = (jnp.where(is_lo, q, zero), jnp.where(is_lo, zero, q))
    c_here = c_ref[qi]
    c_end = (c_here[0:1, tq - 1:tq], c_here[1:2, tq - 1:tq])
    row_id = lax.broadcasted_iota(jnp.int32, (tq, tq), 0)
    col_id = lax.broadcasted_iota(jnp.int32, (tq, tq), 1)

    def step(kj, carry, masked):
        start = pl.multiple_of(kj * tq, tq)
        k = k_ref[pl.ds(start, tq), :]
        v = v_ref[pl.ds(start, tq), :]
        cb = c_ref[kj]
        out = []
        for hh in range(2):
            m, l, acc = carry[hh]
            s = _dot_nt(q_heads[hh], k) + (c_end[hh] - cb[hh:hh + 1, :])
            if masked:
                s = jnp.where(row_id >= col_id, s, NEG_INF)
            m_new = jnp.maximum(m, jnp.max(s, axis=-1, keepdims=True))
            p = jnp.exp(s - m_new)
            alpha = jnp.exp(m - m_new)
            l_new = alpha * l + jnp.sum(p, axis=-1, keepdims=True)
            acc_new = alpha * acc + _dot(p.astype(BF16), v)
            out.append((m_new, l_new, acc_new))
        return tuple(out)

    init_one = (jnp.full((tq, 1), NEG_INF, F32), jnp.zeros((tq, 1), F32),
                jnp.zeros((tq, LANES), F32))
    carry = lax.fori_loop(0, qi, lambda kj, cr: step(kj, cr, False), (init_one, init_one))
    (_, l0, a0), (_, l1, a1) = step(qi, carry, True)
    o_ref[...] = jnp.where(is_lo, a0 * (1.0 / l0), a1 * (1.0 / l1)).astype(o_ref.dtype)


def _fox_attention(aqkv, logf, *, batch, seq, tq):
    tq = min(tq, seq)
    nq = seq // tq
    n_pairs = A_WIDTH // LANES
    qkv = aqkv.reshape(batch, seq, 3 * A_WIDTH)
    return pl.pallas_call(
        functools.partial(_fox_body, tq=tq),
        grid=(batch, n_pairs, nq),
        in_specs=[pl.BlockSpec((None, tq, LANES), lambda b, p, i: (b, i, p)),
                  pl.BlockSpec((None, seq, LANES), lambda b, p, i: (b, 0, n_pairs + p)),
                  pl.BlockSpec((None, seq, LANES), lambda b, p, i: (b, 0, 2 * n_pairs + p)),
                  pl.BlockSpec((None, SUBLANES, seq), lambda b, p, i: (b, 0, 0))],
        out_specs=pl.BlockSpec((None, tq, LANES), lambda b, p, i: (b, i, p)),
        out_shape=jax.ShapeDtypeStruct((batch, seq, A_WIDTH), BF16),
        scratch_shapes=[pltpu.VMEM((nq, SUBLANES, tq), F32)],
        compiler_params=_params(3),
        name="fox_attention",
    )(qkv, qkv, qkv, logf)


def _t5_causal_bucket(dist):
    max_exact = REL_BUCKETS // 2
    large = max_exact + (jnp.log(jnp.maximum(dist, 1).astype(F32) / max_exact)
                         / math.log(REL_MAX_DIST / max_exact)
                         * (REL_BUCKETS - max_exact)).astype(jnp.int32)
    large = jnp.minimum(large, REL_BUCKETS - 1)
    return jnp.where(dist < max_exact, dist, large)


def _band_bias_body(bucket_ref, rel_ref, o_ref):
    bucket = bucket_ref[...]
    for h in range(D_Q_HEADS):
        acc = jnp.zeros(bucket.shape, F32)
        for b in range(REL_BUCKETS):
            acc = jnp.where(bucket == b, rel_ref[b, h], acc)
        o_ref[h] = acc


def _band_bias(rel_bias):
    dist = jnp.maximum(jnp.arange(Q_BLOCK)[:, None] + Q_BLOCK - jnp.arange(2 * Q_BLOCK)[None, :], 0)
    bucket = _t5_causal_bucket(dist).astype(jnp.int32)
    own = jnp.arange(Q_BLOCK)[None, :] <= jnp.arange(Q_BLOCK)[:, None]
    bucket = jnp.where(own, bucket[:, Q_BLOCK:], bucket[:, :Q_BLOCK])
    bias = pl.pallas_call(
        _band_bias_body,
        in_specs=[pl.BlockSpec(memory_space=pltpu.VMEM), pl.BlockSpec(memory_space=pltpu.SMEM)],
        out_specs=pl.BlockSpec(memory_space=pltpu.VMEM),
        out_shape=jax.ShapeDtypeStruct((D_Q_HEADS, Q_BLOCK, Q_BLOCK), F32),
        name="band_bias",
    )(bucket, rel_bias)
    return bias.reshape(D_Q_HEADS * Q_BLOCK, Q_BLOCK)


def _swa_body(q_ref, kp_ref, kc_ref, vp_ref, vc_ref, bias_ref, sink_ref, o_ref, *, nblk):
    step = pl.program_id(1)
    n_lane_blk = D_Q_HEADS // 2
    rows_all = D_Q_HEADS * Q_BLOCK
    is_lo = lax.broadcasted_iota(jnp.int32, (1, LANES), 1) < HEAD_DIM
    row_id = lax.broadcasted_iota(jnp.int32, (rows_all, Q_BLOCK), 0) & (Q_BLOCK - 1)
    col_id = lax.broadcasted_iota(jnp.int32, (rows_all, Q_BLOCK), 1)
    own = col_id <= row_id
    sink = jnp.concatenate([jnp.broadcast_to(sink_ref[h:h + 1, 0:1], (Q_BLOCK, 1))
                            for h in range(D_Q_HEADS)], axis=0)
    bias = bias_ref[...]
    for r in range(nblk):
        rows = slice(r * Q_BLOCK, (r + 1) * Q_BLOCK)
        q_blks = [q_ref[rows, j * LANES:(j + 1) * LANES] for j in range(n_lane_blk)]
        zero = jnp.zeros_like(q_blks[0])
        qq = jnp.concatenate([jnp.where(is_lo, q, zero) for q in q_blks]
                             + [jnp.where(is_lo, zero, q) for q in q_blks], axis=0)
        kc, vc = kc_ref[rows, :], vc_ref[rows, :]
        if r == 0:
            kp, vp = kp_ref[...], vp_ref[...]
        else:
            prev_rows = slice((r - 1) * Q_BLOCK, r * Q_BLOCK)
            kp, vp = kc_ref[prev_rows, :], vc_ref[prev_rows, :]
        s2 = _dot_nt(qq, jnp.concatenate([kc, kp], axis=0))
        s = jnp.where(own, s2[:, :Q_BLOCK], s2[:, Q_BLOCK:]) + bias
        if r == 0:
            s = jnp.where(col_id <= row_id + jnp.where(step > 0, Q_BLOCK, 0), s, NEG_INF)
        m = jnp.maximum(jnp.max(s, axis=-1, keepdims=True), sink)
        e = jnp.exp(s - m)
        inv = 1.0 / (jnp.sum(e, axis=-1, keepdims=True) + jnp.exp(sink - m))
        pr = e * inv
        zf = jnp.zeros_like(pr)
        pp = jnp.concatenate([jnp.where(own, pr, zf), jnp.where(own, zf, pr)], axis=1).astype(BF16)
        o = _dot(pp, jnp.concatenate([vc, vp], axis=0))
        half = n_lane_blk * Q_BLOCK
        for j in range(n_lane_blk):
            lo = o[j * Q_BLOCK:(j + 1) * Q_BLOCK, :]
            hi = o[half + j * Q_BLOCK:half + (j + 1) * Q_BLOCK, :]
            o_ref[rows, j * LANES:(j + 1) * LANES] = jnp.where(is_lo, lo, hi).astype(o_ref.dtype)


def _swa_attention(dqkv, band_bias, sinks, layer, *, batch, seq, nblk):
    nblk = min(nblk, seq // Q_BLOCK)
    rows = nblk * Q_BLOCK
    qkv = dqkv.reshape(batch, seq, D_WIDTH + 2 * D_KV_WIDTH)
    k_blk = D_WIDTH // LANES
    v_blk = k_blk + 1
    prev = lambda t: jnp.maximum(t * nblk - 1, 0)
    return pl.pallas_call(
        functools.partial(_swa_body, nblk=nblk),
        grid=(batch, seq // rows),
        in_specs=[pl.BlockSpec((None, rows, D_WIDTH), lambda b, t: (b, t, 0)),
                  pl.BlockSpec((None, Q_BLOCK, LANES), lambda b, t: (b, prev(t), k_blk)),
                  pl.BlockSpec((None, rows, LANES), lambda b, t: (b, t, k_blk)),
                  pl.BlockSpec((None, Q_BLOCK, LANES), lambda b, t: (b, prev(t), v_blk)),
                  pl.BlockSpec((None, rows, LANES), lambda b, t: (b, t, v_blk)),
                  _const_spec((D_Q_HEADS * Q_BLOCK, Q_BLOCK), (0, 0)),
                  _const_spec((None, D_Q_HEADS, LANES), (layer, 0, 0))],
        out_specs=pl.BlockSpec((None, rows, D_WIDTH), lambda b, t: (b, t, 0)),
        out_shape=jax.ShapeDtypeStruct((batch, seq, D_WIDTH), BF16),
        compiler_params=_params(2),
        name="swa_attention",
    )(qkv, qkv, qkv, qkv, qkv, band_bias, sinks)


def _merge_body(h_ref, ya_ref, yb_ref, yc_ref, yd_ref,
                gpre_ref, wg_ref, bg_ref, wa_ref, wb_ref, wc_ref, wd_ref, wo_ref, gpost_ref, o_ref):
    tm, d = h_ref.shape
    x = h_ref[...]
    u = _rms(x, gpre_ref[...]).astype(BF16)
    branches = ((ya_ref, wa_ref), (yb_ref, wb_ref), (yc_ref, wc_ref), (yd_ref, wd_ref))
    merged = jnp.zeros((tm, d), F32)
    for i, (y_ref, w_ref) in enumerate(branches):
        gate = jax.nn.sigmoid(_dot(u, wg_ref[:, i * d:(i + 1) * d]) + bg_ref[:, i * d:(i + 1) * d])
        merged = merged + gate * _dot(y_ref[...], w_ref[...])
    out = _dot(merged.astype(BF16), wo_ref[...])
    o_ref[...] = x + _rms(out, gpost_ref[...])


def _merge(h, ya, yb, yc, yd, layer, gpre, wg, bg, wa, wb, wc, wd, wo, gpost, *, tm):
    n, d = h.shape
    tm = min(tm, n)
    row = lambda t: (t, 0)
    lay = lambda *shape: _const_spec((None,) + shape, (layer,) + (0,) * len(shape))
    return pl.pallas_call(
        _merge_body,
        grid=(n // tm,),
        in_specs=[pl.BlockSpec((tm, d), row),
                  pl.BlockSpec((tm, A_WIDTH), row), pl.BlockSpec((tm, B_WIDTH), row),
                  pl.BlockSpec((tm, C_WIDTH), row), pl.BlockSpec((tm, D_WIDTH), row),
                  lay(1, d), lay(d, N_BRANCH * d), lay(1, N_BRANCH * d),
                  lay(A_WIDTH, d), lay(B_WIDTH, d), lay(C_WIDTH, d), lay(D_WIDTH, d),
                  lay(d, d), lay(1, d)],
        out_specs=pl.BlockSpec((tm, d), row),
        out_shape=jax.ShapeDtypeStruct((n, d), F32),
        compiler_params=_params(1),
        name="merge",
    )(h, ya.reshape(n, A_WIDTH), yb, yc, yd.reshape(n, D_WIDTH),
      gpre, wg, bg, wa, wb, wc, wd, wo, gpost)


def kernel(x, p, ffn1_norm_pre, ffn1_w_gu, ffn1_w_down, ffn1_norm_post, mix_norm_pre, w_in, b_forget, b_gate, conv_short, conv_dw, conv_dw_bias, conv_ln_gain, conv_ln_bias, attn_sinks, rel_bias, w_br_a, w_br_b, w_br_c, w_br_d, w_o, mix_norm_post, ffn2_norm_pre, ffn2_w_gu, ffn2_w_down, ffn2_norm_post, ple_norm_gate, w_ple_gate, w_ple, ple_norm_post):
    batch, seq, d = x.shape
    depth = w_in.shape[0]
    n = batch * seq
    bf = lambda w: w.astype(BF16)
    vec = lambda g: g.astype(F32)[:, None, :]

    dq = w_in[..., C_END:C_END + D_WIDTH].reshape(depth, d, D_Q_HEADS, HEAD_DIM)
    dq = dq[:, :, D_HEAD_PERM, :].reshape(depth, d, D_WIDTH)
    w_pack = bf(jnp.concatenate(
        [w_in[..., :A_QKV_END], w_in[..., A_F_END:C_END], dq, w_in[..., C_END + D_WIDTH:D_END],
         jnp.pad(w_in[..., A_QKV_END:A_F_END], ((0, 0), (0, 0), (0, LANES - A_HEADS)))], axis=-1))
    w_gate = bf(w_in[..., D_END:])
    b_forget_row = jnp.pad(b_forget.astype(F32), ((0, 0), (0, LANES - A_HEADS)))[:, None, :]
    w_br_d_perm = bf(w_br_d.reshape(depth, D_Q_HEADS, HEAD_DIM, d)[:, D_HEAD_PERM].reshape(depth, D_WIDTH, d))
    sinks = jnp.broadcast_to(attn_sinks.astype(F32)[:, :, None], (depth, D_Q_HEADS, LANES))
    band_bias = _band_bias(rel_bias.astype(F32))

    ffn1_w_gu, ffn1_w_down = bf(ffn1_w_gu), bf(ffn1_w_down)
    ffn2_w_gu, ffn2_w_down = bf(ffn2_w_gu), bf(ffn2_w_down)
    w_br_a, w_br_b, w_br_c, w_o = bf(w_br_a), bf(w_br_b), bf(w_br_c), bf(w_o)
    w_ple_gate, w_ple = bf(w_ple_gate), bf(w_ple)
    p2 = p.reshape(depth, n, p.shape[-1])

    h = x.reshape(n, d)
    for i in range(depth):
        h = _ffn(h, i, vec(ffn1_norm_pre), ffn1_w_gu, ffn1_w_down, vec(ffn1_norm_post), tm=FFN_TM)
        aqkv, yb, yc, dqkv, logf = _inproj(
            h, i, vec(mix_norm_pre), w_pack, b_forget_row, conv_short.astype(F32),
            conv_dw.astype(F32), vec(conv_dw_bias), vec(conv_ln_gain), vec(conv_ln_bias),
            batch=batch, seq=seq, tm=512)
        ya = _fox_attention(aqkv, logf, batch=batch, seq=seq, tq=512)
        yd = _swa_attention(dqkv, band_bias, sinks, i, batch=batch, seq=seq, nblk=4)
        h = _merge(h, ya, yb, yc, yd, i, vec(mix_norm_pre), w_gate, vec(b_gate),
                   w_br_a, w_br_b, w_br_c, w_br_d_perm, w_o, vec(mix_norm_post), tm=512)
        h = _ffn(h, i, vec(ffn2_norm_pre), ffn2_w_gu, ffn2_w_down, vec(ffn2_norm_post),
                 ple=(p2, vec(ple_norm_gate), w_ple_gate, w_ple, vec(ple_norm_post)), tm=FFN_TM)
    return h.reshape(batch, seq, d)
```

```python
import functools
import math

import jax
import jax.numpy as jnp
from jax import lax
from jax.experimental import pallas as pl
from jax.experimental.pallas import tpu as pltpu

F32 = jnp.float32
BF16 = jnp.bfloat16

HEAD_DIM = 64
A_HEADS = 4
A_WIDTH = A_HEADS * HEAD_DIM
B_WIDTH = 256
SHORT_CONV = 3
C_WIDTH = 256
CONF_CONV = 31
D_Q_HEADS = 8
D_KV_HEADS = 2
D_WIDTH = D_Q_HEADS * HEAD_DIM
D_KV_WIDTH = D_KV_HEADS * HEAD_DIM
WINDOW = 128
Q_BLOCK = 128
N_BRANCH = 4
REL_BUCKETS = 32
REL_MAX_DIST = 128
EPS = 1e-6
NEG_INF = -1e30

LANES = 128
SUBLANES = 8
VMEM_LIMIT = 56 * 1024 * 1024
MXU_WIDTH = 256
FFN_TM = 1024
FFN_CHUNK = MXU_WIDTH

A_QKV_END = 3 * A_WIDTH
A_F_END = A_QKV_END + A_HEADS
B_END = A_F_END + 3 * B_WIDTH
C_END = B_END + 2 * C_WIDTH
D_END = C_END + D_WIDTH + 2 * D_KV_WIDTH

D_HEAD_PERM = (0, 4, 1, 5, 2, 6, 3, 7)

P_A = 0
P_B = P_A + 3 * A_WIDTH
P_C = P_B + 3 * B_WIDTH
P_D = P_C + 2 * C_WIDTH
P_F = P_D + D_WIDTH + 2 * D_KV_WIDTH
P_END = P_F + LANES


def _rms(x, g):
    return x * lax.rsqrt(jnp.mean(x * x, axis=-1, keepdims=True) + EPS) * g


def _dot(a, b):
    return jnp.dot(a, b, preferred_element_type=F32)


def _dot_nt(a, b):
    return lax.dot_general(a, b, (((1,), (1,)), ((), ())), preferred_element_type=F32)


def _const_spec(shape, index):
    return pl.BlockSpec(shape, lambda *_: index, pipeline_mode=pl.Buffered(1))


def _params(n_axes):
    return pltpu.CompilerParams(dimension_semantics=("arbitrary",) * n_axes,
                                vmem_limit_bytes=VMEM_LIMIT)


def _ffn_body(*refs, d_ff, n_chunks, has_ple):
    if has_ple:
        (h_ref, p_ref, gpre_ref, wgu_ref, wdn_ref, gpost_ref,
         ggate_ref, wpg_ref, wple_ref, gple_ref, o_ref, act_ref) = refs
    else:
        h_ref, gpre_ref, wgu_ref, wdn_ref, gpost_ref, o_ref, act_ref = refs
    x = h_ref[...]
    u = _rms(x, gpre_ref[...]).astype(BF16)
    fc = d_ff // n_chunks
    for c in range(n_chunks):
        gate = _dot(u, wgu_ref[:, c * fc:(c + 1) * fc])
        up = _dot(u, wgu_ref[:, d_ff + c * fc:d_ff + (c + 1) * fc])
        act_ref[:, c * fc:(c + 1) * fc] = (gate * jax.nn.sigmoid(gate) * up).astype(BF16)
    f = _dot(act_ref[...], wdn_ref[...])
    y = x + 0.5 * _rms(f, gpost_ref[...])
    if has_ple:
        pg = jax.nn.sigmoid(_dot(_rms(y, ggate_ref[...]).astype(BF16), wpg_ref[...]))
        pe = _dot(p_ref[...].astype(BF16), wple_ref[...])
        y = y + pg * _rms(pe, gple_ref[...])
    o_ref[...] = y


def _ffn(h, layer, gpre, wgu, wdn, gpost, ple=None, *, tm):
    n, d = h.shape
    d_ff = wdn.shape[1]
    tm = min(tm, n)
    row = lambda t: (t, 0)
    vec = _const_spec((None, 1, d), (layer, 0, 0))
    in_specs = [pl.BlockSpec((tm, d), row)]
    args = [h]
    if ple is not None:
        p, ggate, wpg, wple, gple = ple
        in_specs.append(pl.BlockSpec((None, tm, p.shape[-1]), lambda t: (layer, t, 0)))
        args.append(p)
    in_specs += [vec, _const_spec((None, d, 2 * d_ff), (layer, 0, 0)),
                 _const_spec((None, d_ff, d), (layer, 0, 0)), vec]
    args += [gpre, wgu, wdn, gpost]
    if ple is not None:
        in_specs += [vec, _const_spec((None, d, d), (layer, 0, 0)),
                     _const_spec((None, p.shape[-1], d), (layer, 0, 0)), vec]
        args += [ggate, wpg, wple, gple]
    return pl.pallas_call(
        functools.partial(_ffn_body, d_ff=d_ff, n_chunks=d_ff // FFN_CHUNK, has_ple=ple is not None),
        grid=(n // tm,),
        in_specs=in_specs,
        out_specs=pl.BlockSpec((tm, d), row),
        out_shape=jax.ShapeDtypeStruct((n, d), F32),
        scratch_shapes=[pltpu.VMEM((tm, d_ff), BF16)],
        compiler_params=_params(1),
        name="ffn_ple" if ple is not None else "ffn",
    )(*args)


CONV_HALO = 32
SHORT_HALO = SUBLANES
CONV_ROWS = 64


def _inproj_body(h_ref, g_ref, w_ref, bf_ref, cs_ref, cdw_ref, cdb_ref, lng_ref, lnb_ref,
                 aqkv_ref, yb_ref, yc_ref, dqkv_ref, logf_ref, u_ref, zext_ref, gext_ref, gsh_ref):
    tm = h_ref.shape[0]

    @pl.when(pl.program_id(1) == 0)
    def _():
        zext_ref[:SHORT_HALO, :] = jnp.zeros((SHORT_HALO, B_WIDTH), F32)
        gext_ref[:CONV_HALO, :] = jnp.zeros((CONV_HALO, C_WIDTH), F32)

    u_ref[...] = _rms(h_ref[...], g_ref[...]).astype(BF16)
    scale = HEAD_DIM ** -0.5

    c = _dot(u_ref[...], w_ref[:, P_C:P_D])
    gext_ref[CONV_HALO:, :] = c[:, :C_WIDTH] * jax.nn.sigmoid(c[:, C_WIDTH:])
    sh_rows = tm + CONV_HALO - SUBLANES
    for s in range(1, SUBLANES):
        gsh_ref[s - 1] = gext_ref[s:s + sh_rows, :]

    def conv_chunk(r0):
        acc = jnp.broadcast_to(cdb_ref[...], (CONV_ROWS, C_WIDTH))
        for k in range(CONF_CONV):
            off = CONV_HALO - (CONF_CONV - 1) + k
            s = off % SUBLANES
            lo = r0 + off - s
            win = gext_ref[lo:lo + CONV_ROWS, :] if s == 0 else gsh_ref[s - 1, lo:lo + CONV_ROWS, :]
            acc = acc + cdw_ref[k:k + 1, :] * win
        mu = jnp.mean(acc, axis=-1, keepdims=True)
        xc = acc - mu
        y = xc * lax.rsqrt(jnp.mean(xc * xc, axis=-1, keepdims=True) + EPS)
        y = y * lng_ref[...] + lnb_ref[...]
        yc_ref[r0:r0 + CONV_ROWS, :] = (y * jax.nn.sigmoid(y)).astype(BF16)

    def proj_a(part):
        cols = slice(part * A_WIDTH, (part + 1) * A_WIDTH)
        a = _dot(u_ref[...], w_ref[:, P_A + part * A_WIDTH:P_A + (part + 1) * A_WIDTH])
        aqkv_ref[:, cols] = ((a * scale) if part == 0 else a).astype(BF16)

    def proj_b():
        b = _dot(u_ref[...], w_ref[:, P_B:P_C])
        zext_ref[SHORT_HALO:, :] = b[:, B_WIDTH:2 * B_WIDTH] * b[:, 2 * B_WIDTH:]
        conv = jnp.zeros((tm, B_WIDTH), F32)
        for k in range(SHORT_CONV):
            off = SHORT_HALO - (SHORT_CONV - 1) + k
            conv = conv + cs_ref[k:k + 1, :] * zext_ref[off:off + tm, :]
        yb_ref[...] = (b[:, :B_WIDTH] * conv).astype(BF16)
        zext_ref[:SHORT_HALO, :] = zext_ref[tm:tm + SHORT_HALO, :]

    def proj_d(part):
        cols = slice(part * MXU_WIDTH, (part + 1) * MXU_WIDTH)
        dd = _dot(u_ref[...], w_ref[:, P_D + part * MXU_WIDTH:P_D + (part + 1) * MXU_WIDTH])
        dqkv_ref[:, cols] = ((dd * scale) if (part + 1) * MXU_WIDTH <= D_WIDTH else dd).astype(BF16)

    def proj_f():
        af = _dot(u_ref[...], w_ref[:, P_F:P_END]) + bf_ref[...]
        t = af.T[:SUBLANES, :]
        logf_ref[...] = jnp.minimum(t, 0.0) - jnp.log1p(jnp.exp(-jnp.abs(t)))

    n_d = (D_WIDTH + 2 * D_KV_WIDTH) // MXU_WIDTH
    projections = ([functools.partial(proj_a, i) for i in range(3)] + [proj_b]
                   + [functools.partial(proj_d, i) for i in range(n_d)] + [proj_f])
    chunks = list(range(0, tm, CONV_ROWS))
    for i in range(max(len(projections), len(chunks))):
        if i < len(projections):
            projections[i]()
        if i < len(chunks):
            conv_chunk(chunks[i])
    gext_ref[:CONV_HALO, :] = gext_ref[tm:tm + CONV_HALO, :]


def _inproj(h, layer, g, w, bf, cs, cdw, cdb, lng, lnb, *, batch, seq, tm):
    n, d = h.shape
    tm = min(tm, seq)
    nt = seq // tm
    row = lambda b, t: (b * nt + t, 0)
    lay = lambda *shape: _const_spec((None,) + shape, (layer,) + (0,) * len(shape))
    out_shape = (jax.ShapeDtypeStruct((n, 3 * A_WIDTH), BF16),
                 jax.ShapeDtypeStruct((n, B_WIDTH), BF16),
                 jax.ShapeDtypeStruct((n, C_WIDTH), BF16),
                 jax.ShapeDtypeStruct((n, D_WIDTH + 2 * D_KV_WIDTH), BF16),
                 jax.ShapeDtypeStruct((batch, SUBLANES, seq), F32))
    out_specs = (pl.BlockSpec((tm, 3 * A_WIDTH), row),
                 pl.BlockSpec((tm, B_WIDTH), row),
                 pl.BlockSpec((tm, C_WIDTH), row),
                 pl.BlockSpec((tm, D_WIDTH + 2 * D_KV_WIDTH), row),
                 pl.BlockSpec((None, SUBLANES, tm), lambda b, t: (b, 0, t)))
    return pl.pallas_call(
        _inproj_body,
        grid=(batch, nt),
        in_specs=[pl.BlockSpec((tm, d), row), lay(1, d), lay(d, P_END), lay(1, LANES),
                  lay(SHORT_CONV, B_WIDTH), lay(CONF_CONV, C_WIDTH), lay(1, C_WIDTH),
                  lay(1, C_WIDTH), lay(1, C_WIDTH)],
        out_specs=out_specs,
        out_shape=out_shape,
        scratch_shapes=[pltpu.VMEM((tm, d), BF16),
                        pltpu.VMEM((tm + SHORT_HALO, B_WIDTH), F32),
                        pltpu.VMEM((tm + CONV_HALO, C_WIDTH), F32),
                        pltpu.VMEM((SUBLANES - 1, tm + CONV_HALO - SUBLANES, C_WIDTH), F32)],
        compiler_params=_params(2),
        name="inproj",
    )(h, g, w, bf, cs, cdw, cdb, lng, lnb)


def _split3(x):
    hi = x.astype(BF16)
    r = x - hi.astype(F32)
    mid = r.astype(BF16)
    lo = (r - mid.astype(F32)).astype(BF16)
    return hi, mid, lo


def _cumsum_lanes(x):
    rows, s = x.shape
    nb = s // LANES
    stacked = jnp.concatenate([x[:, b * LANES:(b + 1) * LANES] for b in range(nb)], axis=0)
    r = lax.broadcasted_iota(jnp.int32, (LANES, LANES), 0)
    c = lax.broadcasted_iota(jnp.int32, (LANES, LANES), 1)
    upper = jnp.where(r <= c, 1.0, 0.0).astype(BF16)
    hi, mid, lo = _split3(stacked)
    within = (_dot(lo, upper) + _dot(mid, upper)) + _dot(hi, upper)
    tot = jnp.broadcast_to(within[:, LANES - 1:], within.shape)
    n = rows * nb
    rr = lax.broadcasted_iota(jnp.int32, (n, n), 0)
    cc = lax.broadcasted_iota(jnp.int32, (n, n), 1)
    shift = rows.bit_length() - 1
    same_row = (rr & (rows - 1)) == (cc & (rows - 1))
    earlier = lax.shift_right_logical(cc, shift) < lax.shift_right_logical(rr, shift)
    prev = jnp.where(same_row & earlier, 1.0, 0.0).astype(BF16)
    hi, mid, lo = _split3(tot)
    total = within + ((_dot(prev, lo) + _dot(prev, mid)) + _dot(prev, hi))
    return [total[b * rows:(b + 1) * rows, :] for b in range(nb)]


def _fox_body(q_ref, k_ref, v_ref, logf_ref, o_ref, c_ref, *, tq, nq):
    pair = pl.program_id(1)
    qi = pl.program_id(2)
    per_blk = tq // LANES

    @pl.when(qi == 0)
    def _():
        blocks = _cumsum_lanes(logf_ref[...])
        for b, blk in enumerate(blocks):
            lanes = slice((b % per_blk) * LANES, (b % per_blk + 1) * LANES)
            c_ref[b // per_blk, :, lanes] = blk
            c_ref[b // per_blk, 0:2, lanes] = jnp.where(pair == 0, blk[0:2, :], blk[2:4, :])

    q = q_ref[...]
    lane = lax.broadcasted_iota(jnp.int32, (1, LANES), 1)
    is_lo = lane < HEAD_DIM
    zero = jnp.zeros_like(q)
    q_heads = (jnp.where(is_lo, q, zero), jnp.where(is_lo, zero, q))
    c_here = c_ref[qi]
    c_end = (c_here[0:1, tq - 1:tq], c_here[1:2, tq - 1:tq])
    row_id = lax.broadcasted_iota(jnp.int32, (tq, tq), 0)
    col_id = lax.broadcasted_iota(jnp.int32, (tq, tq), 1)

    ones = jnp.ones((tq, LANES), BF16)

    def step(kj, carry, masked):
        rows = slice(kj * tq, (kj + 1) * tq)
        k = k_ref[rows, :]
        vv = jnp.concatenate([v_ref[rows, :], ones], axis=1)
        cb = c_ref[kj]
        scores = []
        for hh in range(2):
            s = _dot_nt(q_heads[hh], k) + (c_end[hh] - cb[hh:hh + 1, :])
            if masked:
                s = jnp.where(row_id >= col_id, s, NEG_INF)
            scores.append(s)
        weights, alphas, maxes = [], [], []
        for hh in range(2):
            m = carry[hh][0]
            m_new = jnp.maximum(m, jnp.max(scores[hh], axis=-1, keepdims=True))
            weights.append(jnp.exp(scores[hh] - m_new).astype(BF16))
            alphas.append(jnp.exp(m - m_new))
            maxes.append(m_new)
        return tuple((maxes[hh], alphas[hh] * carry[hh][1] + _dot(weights[hh], vv))
                     for hh in range(2))

    init_one = (jnp.full((tq, 1), NEG_INF, F32), jnp.zeros((tq, 2 * LANES), F32))

    for n_before in range(nq):
        @pl.when(qi == n_before)
        def _(n_before=n_before):
            carry = (init_one, init_one)
            for kj in range(n_before):
                carry = step(kj, carry, False)
            (_, a0), (_, a1) = step(n_before, carry, True)
            o_ref[...] = jnp.where(is_lo, a0[:, :LANES] * (1.0 / a0[:, LANES:]),
                                   a1[:, :LANES] * (1.0 / a1[:, LANES:])).astype(o_ref.dtype)


def _fox_attention(aqkv, logf, *, batch, seq, tq):
    tq = min(tq, seq)
    nq = seq // tq
    n_pairs = A_WIDTH // LANES
    qkv = aqkv.reshape(batch, seq, 3 * A_WIDTH)
    return pl.pallas_call(
        functools.partial(_fox_body, tq=tq, nq=nq),
        grid=(batch, n_pairs, nq),
        in_specs=[pl.BlockSpec((None, tq, LANES), lambda b, p, i: (b, i, p)),
                  pl.BlockSpec((None, seq, LANES), lambda b, p, i: (b, 0, n_pairs + p)),
                  pl.BlockSpec((None, seq, LANES), lambda b, p, i: (b, 0, 2 * n_pairs + p)),
                  pl.BlockSpec((None, SUBLANES, seq), lambda b, p, i: (b, 0, 0))],
        out_specs=pl.BlockSpec((None, tq, LANES), lambda b, p, i: (b, i, p)),
        out_shape=jax.ShapeDtypeStruct((batch, seq, A_WIDTH), BF16),
        scratch_shapes=[pltpu.VMEM((nq, SUBLANES, tq), F32)],
        compiler_params=_params(3),
        name="fox_attention",
    )(qkv, qkv, qkv, logf)


def _t5_causal_bucket(dist):
    max_exact = REL_BUCKETS // 2
    large = max_exact + (jnp.log(jnp.maximum(dist, 1).astype(F32) / max_exact)
                         / math.log(REL_MAX_DIST / max_exact)
                         * (REL_BUCKETS - max_exact)).astype(jnp.int32)
    large = jnp.minimum(large, REL_BUCKETS - 1)
    return jnp.where(dist < max_exact, dist, large)


def _band_bias_body(bucket_ref, rel_ref, o_ref):
    for part in range(2):
        bucket = bucket_ref[part]
        for h in range(D_Q_HEADS):
            acc = jnp.full(bucket.shape, NEG_INF, F32)
            for b in range(REL_BUCKETS):
                acc = jnp.where(bucket == b, rel_ref[b, h], acc)
            o_ref[part, h] = acc


def _band_bias(rel_bias):
    dist = jnp.maximum(jnp.arange(Q_BLOCK)[:, None] + Q_BLOCK - jnp.arange(2 * Q_BLOCK)[None, :], 0)
    bucket = _t5_causal_bucket(dist).astype(jnp.int32)
    own = jnp.arange(Q_BLOCK)[None, :] <= jnp.arange(Q_BLOCK)[:, None]
    bucket = jnp.stack([jnp.where(own, bucket[:, Q_BLOCK:], -1),
                        jnp.where(own, -1, bucket[:, :Q_BLOCK])])
    bias = pl.pallas_call(
        _band_bias_body,
        in_specs=[pl.BlockSpec(memory_space=pltpu.VMEM), pl.BlockSpec(memory_space=pltpu.SMEM)],
        out_specs=pl.BlockSpec(memory_space=pltpu.VMEM),
        out_shape=jax.ShapeDtypeStruct((2, D_Q_HEADS, Q_BLOCK, Q_BLOCK), F32),
        name="band_bias",
    )(bucket, rel_bias)
    own_all = jnp.tile(own, (D_Q_HEADS, 1)).astype(BF16)
    return bias.reshape(2, D_Q_HEADS * Q_BLOCK, Q_BLOCK), own_all


def _swa_body(q_ref, kp_ref, kc_ref, vp_ref, vc_ref, bias_ref, own_ref, sink_ref, o_ref, *, nblk):
    step = pl.program_id(1)
    n_lane_blk = D_Q_HEADS // 2
    is_lo = lax.broadcasted_iota(jnp.int32, (1, LANES), 1) < HEAD_DIM
    sink = jnp.concatenate([jnp.broadcast_to(sink_ref[h:h + 1, :], (Q_BLOCK, LANES))
                            for h in range(D_Q_HEADS)], axis=0)
    ones = jnp.ones((Q_BLOCK, LANES), BF16)
    no_prev = jnp.where(step > 0, 0.0, NEG_INF)
    def block_rows(r):
        return slice(r * Q_BLOCK, (r + 1) * Q_BLOCK)

    scores = []
    for r in range(nblk):
        rows = block_rows(r)
        q_blks = [q_ref[rows, j * LANES:(j + 1) * LANES] for j in range(n_lane_blk)]
        zero = jnp.zeros_like(q_blks[0])
        qq = jnp.concatenate([jnp.where(is_lo, q, zero) for q in q_blks]
                             + [jnp.where(is_lo, zero, q) for q in q_blks], axis=0)
        kp = kp_ref[...] if r == 0 else kc_ref[block_rows(r - 1), :]
        scores.append(_dot_nt(qq, jnp.concatenate([kc_ref[rows, :], kp], axis=0)))
    weights, maxes = [], []
    for r in range(nblk):
        s2 = scores[r]
        s_own = s2[:, :Q_BLOCK] + bias_ref[0]
        s_prev = s2[:, Q_BLOCK:] + bias_ref[1]
        if r == 0:
            s_prev = s_prev + no_prev
        s = jnp.maximum(s_own, s_prev)
        m = jnp.max(jnp.maximum(s, sink), axis=-1, keepdims=True)
        e = jnp.exp(s - m).astype(BF16)
        e_own = e * own_ref[...]
        weights.append(jnp.concatenate([e_own, e - e_own], axis=1))
        maxes.append(m)
    outs = []
    for r in range(nblk):
        rows = block_rows(r)
        vp = vp_ref[...] if r == 0 else vc_ref[block_rows(r - 1), :]
        vv = jnp.concatenate([jnp.concatenate([vc_ref[rows, :], ones], axis=1),
                              jnp.concatenate([vp, ones], axis=1)], axis=0)
        outs.append(_dot(weights[r], vv))
    half = n_lane_blk * Q_BLOCK
    for r in range(nblk):
        o2 = outs[r]
        o = o2[:, :LANES] * (1.0 / (o2[:, LANES:] + jnp.exp(sink - maxes[r])))
        for j in range(n_lane_blk):
            lo = o[j * Q_BLOCK:(j + 1) * Q_BLOCK, :]
            hi = o[half + j * Q_BLOCK:half + (j + 1) * Q_BLOCK, :]
            o_ref[block_rows(r), j * LANES:(j + 1) * LANES] = jnp.where(is_lo, lo, hi).astype(o_ref.dtype)


def _swa_attention(dqkv, band_bias, own, sinks, layer, *, batch, seq, nblk):
    nblk = min(nblk, seq // Q_BLOCK)
    rows = nblk * Q_BLOCK
    qkv = dqkv.reshape(batch, seq, D_WIDTH + 2 * D_KV_WIDTH)
    k_blk = D_WIDTH // LANES
    v_blk = k_blk + 1
    prev = lambda t: jnp.maximum(t * nblk - 1, 0)
    return pl.pallas_call(
        functools.partial(_swa_body, nblk=nblk),
        grid=(batch, seq // rows),
        in_specs=[pl.BlockSpec((None, rows, D_WIDTH), lambda b, t: (b, t, 0)),
                  pl.BlockSpec((None, Q_BLOCK, LANES), lambda b, t: (b, prev(t), k_blk)),
                  pl.BlockSpec((None, rows, LANES), lambda b, t: (b, t, k_blk)),
                  pl.BlockSpec((None, Q_BLOCK, LANES), lambda b, t: (b, prev(t), v_blk)),
                  pl.BlockSpec((None, rows, LANES), lambda b, t: (b, t, v_blk)),
                  _const_spec((2, D_Q_HEADS * Q_BLOCK, Q_BLOCK), (0, 0, 0)),
                  _const_spec((D_Q_HEADS * Q_BLOCK, Q_BLOCK), (0, 0)),
                  _const_spec((None, D_Q_HEADS, LANES), (layer, 0, 0))],
        out_specs=pl.BlockSpec((None, rows, D_WIDTH), lambda b, t: (b, t, 0)),
        out_shape=jax.ShapeDtypeStruct((batch, seq, D_WIDTH), BF16),
        compiler_params=_params(2),
        name="swa_attention",
    )(qkv, qkv, qkv, qkv, qkv, band_bias, own, sinks)


def _merge_body(h_ref, ya_ref, yb_ref, yc_ref, yd_ref,
                gpre_ref, wg_ref, bg_ref, wa_ref, wb_ref, wc_ref, wd_ref, wo_ref, gpost_ref, o_ref):
    tm, d = h_ref.shape
    x = h_ref[...]
    u = _rms(x, gpre_ref[...]).astype(BF16)
    branches = ((ya_ref, wa_ref), (yb_ref, wb_ref), (yc_ref, wc_ref), (yd_ref, wd_ref))
    merged = jnp.zeros((tm, d), F32)
    for i, (y_ref, w_ref) in enumerate(branches):
        gate = jax.nn.sigmoid(_dot(u, wg_ref[:, i * d:(i + 1) * d]) + bg_ref[:, i * d:(i + 1) * d])
        merged = merged + gate * _dot(y_ref[...], w_ref[...])
    out = _dot(merged.astype(BF16), wo_ref[...])
    o_ref[...] = x + _rms(out, gpost_ref[...])


def _merge(h, ya, yb, yc, yd, layer, gpre, wg, bg, wa, wb, wc, wd, wo, gpost, *, tm):
    n, d = h.shape
    tm = min(tm, n)
    row = lambda t: (t, 0)
    lay = lambda *shape: _const_spec((None,) + shape, (layer,) + (0,) * len(shape))
    return pl.pallas_call(
        _merge_body,
        grid=(n // tm,),
        in_specs=[pl.BlockSpec((tm, d), row),
                  pl.BlockSpec((tm, A_WIDTH), row), pl.BlockSpec((tm, B_WIDTH), row),
                  pl.BlockSpec((tm, C_WIDTH), row), pl.BlockSpec((tm, D_WIDTH), row),
                  lay(1, d), lay(d, N_BRANCH * d), lay(1, N_BRANCH * d),
                  lay(A_WIDTH, d), lay(B_WIDTH, d), lay(C_WIDTH, d), lay(D_WIDTH, d),
                  lay(d, d), lay(1, d)],
        out_specs=pl.BlockSpec((tm, d), row),
        out_shape=jax.ShapeDtypeStruct((n, d), F32),
        compiler_params=_params(1),
        name="merge",
    )(h, ya.reshape(n, A_WIDTH), yb, yc, yd.reshape(n, D_WIDTH),
      gpre, wg, bg, wa, wb, wc, wd, wo, gpost)


def kernel(x, p, ffn1_norm_pre, ffn1_w_gu, ffn1_w_down, ffn1_norm_post, mix_norm_pre, w_in, b_forget, b_gate, conv_short, conv_dw, conv_dw_bias, conv_ln_gain, conv_ln_bias, attn_sinks, rel_bias, w_br_a, w_br_b, w_br_c, w_br_d, w_o, mix_norm_post, ffn2_norm_pre, ffn2_w_gu, ffn2_w_down, ffn2_norm_post, ple_norm_gate, w_ple_gate, w_ple, ple_norm_post):
    batch, seq, d = x.shape
    depth = w_in.shape[0]
    n = batch * seq
    bf = lambda w: w.astype(BF16)
    vec = lambda g: g.astype(F32)[:, None, :]

    dq = w_in[..., C_END:C_END + D_WIDTH].reshape(depth, d, D_Q_HEADS, HEAD_DIM)
    dq = dq[:, :, D_HEAD_PERM, :].reshape(depth, d, D_WIDTH)
    w_pack = bf(jnp.concatenate(
        [w_in[..., :A_QKV_END], w_in[..., A_F_END:C_END], dq, w_in[..., C_END + D_WIDTH:D_END],
         jnp.pad(w_in[..., A_QKV_END:A_F_END], ((0, 0), (0, 0), (0, LANES - A_HEADS)))], axis=-1))
    w_gate = bf(w_in[..., D_END:])
    b_forget_row = jnp.pad(b_forget.astype(F32), ((0, 0), (0, LANES - A_HEADS)))[:, None, :]
    w_br_d_perm = bf(w_br_d.reshape(depth, D_Q_HEADS, HEAD_DIM, d)[:, D_HEAD_PERM].reshape(depth, D_WIDTH, d))
    sinks = jnp.broadcast_to(attn_sinks.astype(F32)[:, :, None], (depth, D_Q_HEADS, LANES))
    band_bias, band_own = _band_bias(rel_bias.astype(F32))

    ffn1_w_gu, ffn1_w_down = bf(ffn1_w_gu), bf(ffn1_w_down)
    ffn2_w_gu, ffn2_w_down = bf(ffn2_w_gu), bf(ffn2_w_down)
    w_br_a, w_br_b, w_br_c, w_o = bf(w_br_a), bf(w_br_b), bf(w_br_c), bf(w_o)
    w_ple_gate, w_ple = bf(w_ple_gate), bf(w_ple)
    p2 = p.reshape(depth, n, p.shape[-1])

    h = x.reshape(n, d)
    for i in range(depth):
        h = _ffn(h, i, vec(ffn1_norm_pre), ffn1_w_gu, ffn1_w_down, vec(ffn1_norm_post), tm=FFN_TM)
        aqkv, yb, yc, dqkv, logf = _inproj(
            h, i, vec(mix_norm_pre), w_pack, b_forget_row, conv_short.astype(F32),
            conv_dw.astype(F32), vec(conv_dw_bias), vec(conv_ln_gain), vec(conv_ln_bias),
            batch=batch, seq=seq, tm=512)
        ya = _fox_attention(aqkv, logf, batch=batch, seq=seq, tq=512)
        yd = _swa_attention(dqkv, band_bias, band_own, sinks, i, batch=batch, seq=seq, nblk=4)
        h = _merge(h, ya, yb, yc, yd, i, vec(mix_norm_pre), w_gate, vec(b_gate),
                   w_br_a, w_br_b, w_br_c, w_br_d_perm, w_o, vec(mix_norm_post), tm=512)
        h = _ffn(h, i, vec(ffn2_norm_pre), ffn2_w_gu, ffn2_w_down, vec(ffn2_norm_post),
                 ple=(p2, vec(ple_norm_gate), w_ple_gate, w_ple, vec(ple_norm_post)), tm=FFN_TM)
    return h.reshape(batch, seq, d)
```

```python
import functools
import math

import jax
import jax.numpy as jnp
from jax import lax
from jax.experimental import pallas as pl
from jax.experimental.pallas import tpu as pltpu

F32 = jnp.float32
BF16 = jnp.bfloat16

HEAD_DIM = 64
A_HEADS = 4
A_WIDTH = A_HEADS * HEAD_DIM
B_WIDTH = 256
SHORT_CONV = 3
C_WIDTH = 256
CONF_CONV = 31
D_Q_HEADS = 8
D_KV_HEADS = 2
D_WIDTH = D_Q_HEADS * HEAD_DIM
D_KV_WIDTH = D_KV_HEADS * HEAD_DIM
WINDOW = 128
Q_BLOCK = 128
N_BRANCH = 4
REL_BUCKETS = 32
REL_MAX_DIST = 128
EPS = 1e-6
NEG_INF = -1e30

LANES = 128
SUBLANES = 8
VMEM_LIMIT = 56 * 1024 * 1024
MXU_WIDTH = 256
FFN_TM = 1024
FFN_CHUNK = MXU_WIDTH
FFN_SPLIT = 2

A_QKV_END = 3 * A_WIDTH
A_F_END = A_QKV_END + A_HEADS
B_END = A_F_END + 3 * B_WIDTH
C_END = B_END + 2 * C_WIDTH
D_END = C_END + D_WIDTH + 2 * D_KV_WIDTH

D_HEAD_PERM = (0, 4, 1, 5, 2, 6, 3, 7)

P_A = 0
P_B = P_A + 3 * A_WIDTH
P_C = P_B + 3 * B_WIDTH
P_D = P_C + 2 * C_WIDTH
P_F = P_D + D_WIDTH + 2 * D_KV_WIDTH
P_END = P_F + LANES


def _rms(x, g):
    return x * lax.rsqrt(jnp.mean(x * x, axis=-1, keepdims=True) + EPS) * g


def _dot(a, b):
    return jnp.dot(a, b, preferred_element_type=F32)


def _dot_nt(a, b):
    return lax.dot_general(a, b, (((1,), (1,)), ((), ())), preferred_element_type=F32)


def _zero_after(x):
    bits = lax.bitcast_convert_type(x, jnp.uint32)
    sixteen = jnp.uint32(16)
    bits = lax.shift_right_logical(lax.shift_right_logical(bits, sixteen), sixteen)
    return lax.bitcast_convert_type(bits, F32)


def _const_spec(shape, index):
    return pl.BlockSpec(shape, lambda *_: index, pipeline_mode=pl.Buffered(1))


def _params(n_axes):
    return pltpu.CompilerParams(dimension_semantics=("arbitrary",) * n_axes,
                                vmem_limit_bytes=VMEM_LIMIT)


def _ffn_body(*refs, d_ff, n_chunks, has_ple):
    if has_ple:
        (h_ref, p_ref, gpre_ref, wgu_ref, wdn_ref, gpost_ref,
         ggate_ref, wpg_ref, wple_ref, gple_ref, o_ref, act_ref) = refs
    else:
        h_ref, gpre_ref, wgu_ref, wdn_ref, gpost_ref, o_ref, act_ref = refs
    tm = h_ref.shape[0]
    halves = [slice(i * tm // FFN_SPLIT, (i + 1) * tm // FFN_SPLIT) for i in range(FFN_SPLIT)]
    xs = [h_ref[r, :] for r in halves]
    us = [_rms(x, gpre_ref[...]).astype(BF16) for x in xs]
    fc = d_ff // n_chunks
    for c in range(n_chunks):
        for r, u in zip(halves, us):
            gate = _dot(u, wgu_ref[:, c * fc:(c + 1) * fc])
            up = _dot(u, wgu_ref[:, d_ff + c * fc:d_ff + (c + 1) * fc])
            act_ref[r, c * fc:(c + 1) * fc] = (gate * jax.nn.sigmoid(gate) * up).astype(BF16)
    fs = [_dot(act_ref[r, :], wdn_ref[...]) for r in halves]
    ys = [x + 0.5 * _rms(f, gpost_ref[...]) for x, f in zip(xs, fs)]
    if has_ple:
        pes = [_dot(p_ref[r, :].astype(BF16), wple_ref[...]) for r in halves]
        pgs = [jax.nn.sigmoid(_dot(_rms(y, ggate_ref[...]).astype(BF16), wpg_ref[...])) for y in ys]
        ys = [y + pg * _rms(pe, gple_ref[...]) for y, pg, pe in zip(ys, pgs, pes)]
    for r, y in zip(halves, ys):
        o_ref[r, :] = y


def _ffn(h, layer, gpre, wgu, wdn, gpost, ple=None, *, tm):
    n, d = h.shape
    d_ff = wdn.shape[1]
    tm = min(tm, n)
    row = lambda t: (t, 0)
    vec = _const_spec((None, 1, d), (layer, 0, 0))
    in_specs = [pl.BlockSpec((tm, d), row)]
    args = [h]
    if ple is not None:
        p, ggate, wpg, wple, gple = ple
        in_specs.append(pl.BlockSpec((None, tm, p.shape[-1]), lambda t: (layer, t, 0)))
        args.append(p)
    in_specs += [vec, _const_spec((None, d, 2 * d_ff), (layer, 0, 0)),
                 _const_spec((None, d_ff, d), (layer, 0, 0)), vec]
    args += [gpre, wgu, wdn, gpost]
    if ple is not None:
        in_specs += [vec, _const_spec((None, d, d), (layer, 0, 0)),
                     _const_spec((None, p.shape[-1], d), (layer, 0, 0)), vec]
        args += [ggate, wpg, wple, gple]
    return pl.pallas_call(
        functools.partial(_ffn_body, d_ff=d_ff, n_chunks=d_ff // FFN_CHUNK, has_ple=ple is not None),
        grid=(n // tm,),
        in_specs=in_specs,
        out_specs=pl.BlockSpec((tm, d), row),
        out_shape=jax.ShapeDtypeStruct((n, d), F32),
        scratch_shapes=[pltpu.VMEM((tm, d_ff), BF16)],
        compiler_params=_params(1),
        name="ffn_ple" if ple is not None else "ffn",
    )(*args)


CONV_HALO = 32
SHORT_HALO = SUBLANES
CONV_ROWS = 64
CONV_SPLIT = 1
CONV_TAP_ROWS = 32


def _inproj_body(h_ref, g_ref, w_ref, bf_ref, cs_ref, cdw_ref, cdb_ref, lng_ref, lnb_ref,
                 aqkv_ref, yb_ref, yc_ref, dqkv_ref, logf_ref,
                 u_ref, zext_ref, gext_ref, gsh_ref, taps_ref):
    tm = h_ref.shape[0]

    @pl.when(pl.program_id(1) == 0)
    def _():
        zext_ref[:SHORT_HALO, :] = jnp.zeros((SHORT_HALO, B_WIDTH), F32)
        gext_ref[:CONV_HALO, :] = jnp.zeros((CONV_HALO, C_WIDTH), F32)

    u_ref[...] = _rms(h_ref[...], g_ref[...]).astype(BF16)

    c = _dot(u_ref[...], w_ref[:, P_C:P_D])
    gext_ref[CONV_HALO:, :] = c[:, :C_WIDTH] * jax.nn.sigmoid(c[:, C_WIDTH:])
    sh_rows = tm + CONV_HALO - SUBLANES
    for s in range(1, SUBLANES):
        gsh_ref[s - 1] = gext_ref[s:s + sh_rows, :]

    def conv_chunk(r0, gate):
        sub = CONV_ROWS // CONV_SPLIT
        for j in range(CONV_SPLIT):
            g_row = (j + 1) * tm // CONV_SPLIT
            zero = _zero_after(gate[g_row - SUBLANES:g_row, :LANES])
            taps_ref[...] = cdw_ref[...] + jnp.tile(zero, (CONV_TAP_ROWS // SUBLANES, C_WIDTH // LANES))
            base = r0 + j * sub
            acc = jnp.broadcast_to(cdb_ref[...], (sub, C_WIDTH))
            for k in range(CONF_CONV):
                off = CONV_HALO - (CONF_CONV - 1) + k
                s = off % SUBLANES
                lo = base + off - s
                win = gext_ref[lo:lo + sub, :] if s == 0 else gsh_ref[s - 1, lo:lo + sub, :]
                acc = acc + taps_ref[k:k + 1, :] * win
            mu = jnp.mean(acc, axis=-1, keepdims=True)
            xc = acc - mu
            y = xc * lax.rsqrt(jnp.mean(xc * xc, axis=-1, keepdims=True) + EPS)
            y = y * lng_ref[...] + lnb_ref[...]
            yc_ref[base:base + sub, :] = (y * jax.nn.sigmoid(y)).astype(BF16)

    def slab(lo):
        return _dot(u_ref[...], w_ref[:, lo:lo + MXU_WIDTH])

    def proj_a(part):
        a = slab(P_A + part * A_WIDTH)
        aqkv_ref[:, part * A_WIDTH:(part + 1) * A_WIDTH] = a.astype(BF16)
        return a

    held = {}

    def proj_b(part):
        b = slab(P_B + part * B_WIDTH)
        if part < 2:
            held[part] = b
            return b
        zext_ref[SHORT_HALO:, :] = held[1] * b
        conv = jnp.zeros((tm, B_WIDTH), F32)
        for k in range(SHORT_CONV):
            off = SHORT_HALO - (SHORT_CONV - 1) + k
            conv = conv + cs_ref[k:k + 1, :] * zext_ref[off:off + tm, :]
        yb_ref[...] = (held[0] * conv).astype(BF16)
        zext_ref[:SHORT_HALO, :] = zext_ref[tm:tm + SHORT_HALO, :]
        return b

    def proj_d(part):
        dd = slab(P_D + part * MXU_WIDTH)
        dqkv_ref[:, part * MXU_WIDTH:(part + 1) * MXU_WIDTH] = dd.astype(BF16)
        return dd

    def proj_f():
        af = _dot(u_ref[...], w_ref[:, P_F:P_END]) + bf_ref[...]
        t = af.T[:SUBLANES, :]
        logf_ref[...] = jnp.minimum(t, 0.0) - jnp.log1p(jnp.exp(-jnp.abs(t)))
        return af

    n_d = (D_WIDTH + 2 * D_KV_WIDTH) // MXU_WIDTH
    projections = ([functools.partial(proj_a, i) for i in range(3)]
                   + [functools.partial(proj_b, i) for i in range(3)]
                   + [functools.partial(proj_d, i) for i in range(n_d)] + [proj_f])
    chunks = list(range(0, tm, CONV_ROWS))
    assert len(chunks) <= len(projections)
    for i, proj in enumerate(projections):
        res = proj()
        if i < len(chunks):
            conv_chunk(chunks[i], res)
    gext_ref[:CONV_HALO, :] = gext_ref[tm:tm + CONV_HALO, :]


def _inproj(h, layer, g, w, bf, cs, cdw, cdb, lng, lnb, *, batch, seq, tm):
    n, d = h.shape
    tm = min(tm, seq)
    nt = seq // tm
    row = lambda b, t: (b * nt + t, 0)
    lay = lambda *shape: _const_spec((None,) + shape, (layer,) + (0,) * len(shape))
    out_shape = (jax.ShapeDtypeStruct((n, 3 * A_WIDTH), BF16),
                 jax.ShapeDtypeStruct((n, B_WIDTH), BF16),
                 jax.ShapeDtypeStruct((n, C_WIDTH), BF16),
                 jax.ShapeDtypeStruct((n, D_WIDTH + 2 * D_KV_WIDTH), BF16),
                 jax.ShapeDtypeStruct((batch, SUBLANES, seq), F32))
    out_specs = (pl.BlockSpec((tm, 3 * A_WIDTH), row),
                 pl.BlockSpec((tm, B_WIDTH), row),
                 pl.BlockSpec((tm, C_WIDTH), row),
                 pl.BlockSpec((tm, D_WIDTH + 2 * D_KV_WIDTH), row),
                 pl.BlockSpec((None, SUBLANES, tm), lambda b, t: (b, 0, t)))
    return pl.pallas_call(
        _inproj_body,
        grid=(batch, nt),
        in_specs=[pl.BlockSpec((tm, d), row), lay(1, d), lay(d, P_END), lay(1, LANES),
                  lay(SHORT_CONV, B_WIDTH), lay(CONV_TAP_ROWS, C_WIDTH), lay(1, C_WIDTH),
                  lay(1, C_WIDTH), lay(1, C_WIDTH)],
        out_specs=out_specs,
        out_shape=out_shape,
        scratch_shapes=[pltpu.VMEM((tm, d), BF16),
                        pltpu.VMEM((tm + SHORT_HALO, B_WIDTH), F32),
                        pltpu.VMEM((tm + CONV_HALO, C_WIDTH), F32),
                        pltpu.VMEM((SUBLANES - 1, tm + CONV_HALO - SUBLANES, C_WIDTH), F32),
                        pltpu.VMEM((CONV_TAP_ROWS, C_WIDTH), F32)],
        compiler_params=_params(2),
        name="inproj",
    )(h, g, w, bf, cs, cdw, cdb, lng, lnb)


def _split3(x):
    hi = x.astype(BF16)
    r = x - hi.astype(F32)
    mid = r.astype(BF16)
    lo = (r - mid.astype(F32)).astype(BF16)
    return hi, mid, lo


def _cumsum_lanes(x):
    rows, s = x.shape
    nb = s // LANES
    stacked = jnp.concatenate([x[:, b * LANES:(b + 1) * LANES] for b in range(nb)], axis=0)
    r = lax.broadcasted_iota(jnp.int32, (LANES, LANES), 0)
    c = lax.broadcasted_iota(jnp.int32, (LANES, LANES), 1)
    upper = jnp.where(r <= c, 1.0, 0.0).astype(BF16)
    hi, mid, lo = _split3(stacked)
    within = (_dot(lo, upper) + _dot(mid, upper)) + _dot(hi, upper)
    tot = jnp.broadcast_to(within[:, LANES - 1:], within.shape)
    n = rows * nb
    rr = lax.broadcasted_iota(jnp.int32, (n, n), 0)
    cc = lax.broadcasted_iota(jnp.int32, (n, n), 1)
    shift = rows.bit_length() - 1
    same_row = (rr & (rows - 1)) == (cc & (rows - 1))
    earlier = lax.shift_right_logical(cc, shift) < lax.shift_right_logical(rr, shift)
    prev = jnp.where(same_row & earlier, 1.0, 0.0).astype(BF16)
    hi, mid, lo = _split3(tot)
    total = within + ((_dot(prev, lo) + _dot(prev, mid)) + _dot(prev, hi))
    return [total[b * rows:(b + 1) * rows, :] for b in range(nb)]


def _fox_body(q_ref, k_ref, v_ref, logf_ref, o_ref, c_ref, *, tq, nq):
    pair = pl.program_id(1)
    qi = pl.program_id(2)
    per_blk = tq // LANES

    @pl.when(qi == 0)
    def _():
        blocks = _cumsum_lanes(logf_ref[...])
        for b, blk in enumerate(blocks):
            lanes = slice((b % per_blk) * LANES, (b % per_blk + 1) * LANES)
            c_ref[b // per_blk, :, lanes] = blk
            c_ref[b // per_blk, 0:2, lanes] = jnp.where(pair == 0, blk[0:2, :], blk[2:4, :])

    q = q_ref[...]
    lane = lax.broadcasted_iota(jnp.int32, (1, LANES), 1)
    is_lo = lane < HEAD_DIM
    zero = jnp.zeros_like(q)
    q_heads = (jnp.where(is_lo, q, zero), jnp.where(is_lo, zero, q))
    c_here = c_ref[qi]
    c_end = (c_here[0:1, tq - 1:tq], c_here[1:2, tq - 1:tq])
    row_id = lax.broadcasted_iota(jnp.int32, (tq, tq), 0)
    col_id = lax.broadcasted_iota(jnp.int32, (tq, tq), 1)

    ones = jnp.ones((tq, LANES), BF16)

    def step(kj, carry, masked):
        rows = slice(kj * tq, (kj + 1) * tq)
        k = k_ref[rows, :]
        vv = jnp.concatenate([v_ref[rows, :], ones], axis=1)
        cb = c_ref[kj]
        scores = []
        for hh in range(2):
            s = _dot_nt(q_heads[hh], k) + (c_end[hh] - cb[hh:hh + 1, :])
            if masked:
                s = jnp.where(row_id >= col_id, s, NEG_INF)
            scores.append(s)
        weights, alphas, maxes = [], [], []
        for hh in range(2):
            m = carry[hh][0]
            m_new = jnp.maximum(m, jnp.max(scores[hh], axis=-1, keepdims=True))
            weights.append(jnp.exp(scores[hh] - m_new).astype(BF16))
            alphas.append(jnp.exp(m - m_new))
            maxes.append(m_new)
        return tuple((maxes[hh], alphas[hh] * carry[hh][1] + _dot(weights[hh], vv))
                     for hh in range(2))

    init_one = (jnp.full((tq, 1), NEG_INF, F32), jnp.zeros((tq, 2 * LANES), F32))

    for n_before in range(nq):
        @pl.when(qi == n_before)
        def _(n_before=n_before):
            carry = (init_one, init_one)
            for kj in range(n_before):
                carry = step(kj, carry, False)
            (_, a0), (_, a1) = step(n_before, carry, True)
            o_ref[...] = jnp.where(is_lo, a0[:, :LANES] * (1.0 / a0[:, LANES:]),
                                   a1[:, :LANES] * (1.0 / a1[:, LANES:])).astype(o_ref.dtype)


def _fox_attention(aqkv, logf, *, batch, seq, tq):
    tq = min(tq, seq)
    nq = seq // tq
    n_pairs = A_WIDTH // LANES
    qkv = aqkv.reshape(batch, seq, 3 * A_WIDTH)
    return pl.pallas_call(
        functools.partial(_fox_body, tq=tq, nq=nq),
        grid=(batch, n_pairs, nq),
        in_specs=[pl.BlockSpec((None, tq, LANES), lambda b, p, i: (b, i, p)),
                  pl.BlockSpec((None, seq, LANES), lambda b, p, i: (b, 0, n_pairs + p)),
                  pl.BlockSpec((None, seq, LANES), lambda b, p, i: (b, 0, 2 * n_pairs + p)),
                  pl.BlockSpec((None, SUBLANES, seq), lambda b, p, i: (b, 0, 0))],
        out_specs=pl.BlockSpec((None, tq, LANES), lambda b, p, i: (b, i, p)),
        out_shape=jax.ShapeDtypeStruct((batch, seq, A_WIDTH), BF16),
        scratch_shapes=[pltpu.VMEM((nq, SUBLANES, tq), F32)],
        compiler_params=_params(3),
        name="fox_attention",
    )(qkv, qkv, qkv, logf)


def _t5_causal_bucket(dist):
    max_exact = REL_BUCKETS // 2
    large = max_exact + (jnp.log(jnp.maximum(dist, 1).astype(F32) / max_exact)
                         / math.log(REL_MAX_DIST / max_exact)
                         * (REL_BUCKETS - max_exact)).astype(jnp.int32)
    large = jnp.minimum(large, REL_BUCKETS - 1)
    return jnp.where(dist < max_exact, dist, large)


def _band_bias_body(bucket_ref, rel_ref, o_ref):
    for part in range(2):
        bucket = bucket_ref[part]
        for h in range(D_Q_HEADS):
            acc = jnp.full(bucket.shape, NEG_INF, F32)
            for b in range(REL_BUCKETS):
                acc = jnp.where(bucket == b, rel_ref[b, h], acc)
            o_ref[part, h] = acc


def _band_bias(rel_bias):
    dist = jnp.maximum(jnp.arange(Q_BLOCK)[:, None] + Q_BLOCK - jnp.arange(2 * Q_BLOCK)[None, :], 0)
    bucket = _t5_causal_bucket(dist).astype(jnp.int32)
    own = jnp.arange(Q_BLOCK)[None, :] <= jnp.arange(Q_BLOCK)[:, None]
    bucket = jnp.stack([jnp.where(own, bucket[:, Q_BLOCK:], -1),
                        jnp.where(own, -1, bucket[:, :Q_BLOCK])])
    bias = pl.pallas_call(
        _band_bias_body,
        in_specs=[pl.BlockSpec(memory_space=pltpu.VMEM), pl.BlockSpec(memory_space=pltpu.SMEM)],
        out_specs=pl.BlockSpec(memory_space=pltpu.VMEM),
        out_shape=jax.ShapeDtypeStruct((2, D_Q_HEADS, Q_BLOCK, Q_BLOCK), F32),
        name="band_bias",
    )(bucket, rel_bias)
    own_all = jnp.tile(own, (D_Q_HEADS, 1)).astype(BF16)
    return bias.reshape(2, D_Q_HEADS * Q_BLOCK, Q_BLOCK), own_all


def _swa_body(q_ref, kp_ref, kc_ref, vp_ref, vc_ref, bias_ref, own_ref, sink_ref, o_ref, *, nblk):
    step = pl.program_id(1)
    n_lane_blk = D_Q_HEADS // 2
    is_lo = lax.broadcasted_iota(jnp.int32, (1, LANES), 1) < HEAD_DIM
    sink = jnp.concatenate([jnp.broadcast_to(sink_ref[h:h + 1, :], (Q_BLOCK, LANES))
                            for h in range(D_Q_HEADS)], axis=0)
    ones = jnp.ones((Q_BLOCK, LANES), BF16)
    no_prev = jnp.where(step > 0, 0.0, NEG_INF)
    def block_rows(r):
        return slice(r * Q_BLOCK, (r + 1) * Q_BLOCK)

    scores = []
    for r in range(nblk):
        rows = block_rows(r)
        q_blks = [q_ref[rows, j * LANES:(j + 1) * LANES] for j in range(n_lane_blk)]
        zero = jnp.zeros_like(q_blks[0])
        qq = jnp.concatenate([jnp.where(is_lo, q, zero) for q in q_blks]
                             + [jnp.where(is_lo, zero, q) for q in q_blks], axis=0)
        kp = kp_ref[...] if r == 0 else kc_ref[block_rows(r - 1), :]
        scores.append(_dot_nt(qq, jnp.concatenate([kc_ref[rows, :], kp], axis=0)))
    weights, maxes = [], []
    for r in range(nblk):
        s2 = scores[r]
        s_own = s2[:, :Q_BLOCK] + bias_ref[0]
        s_prev = s2[:, Q_BLOCK:] + bias_ref[1]
        if r == 0:
            s_prev = s_prev + no_prev
        s = jnp.maximum(s_own, s_prev)
        m = jnp.max(jnp.maximum(s, sink), axis=-1, keepdims=True)
        e = jnp.exp(s - m).astype(BF16)
        e_own = e * own_ref[...]
        weights.append(jnp.concatenate([e_own, e - e_own], axis=1))
        maxes.append(m)
    outs = []
    for r in range(nblk):
        rows = block_rows(r)
        vp = vp_ref[...] if r == 0 else vc_ref[block_rows(r - 1), :]
        vv = jnp.concatenate([jnp.concatenate([vc_ref[rows, :], ones], axis=1),
                              jnp.concatenate([vp, ones], axis=1)], axis=0)
        outs.append(_dot(weights[r], vv))
    half = n_lane_blk * Q_BLOCK
    for r in range(nblk):
        o2 = outs[r]
        o = o2[:, :LANES] * (1.0 / (o2[:, LANES:] + jnp.exp(sink - maxes[r])))
        for j in range(n_lane_blk):
            lo = o[j * Q_BLOCK:(j + 1) * Q_BLOCK, :]
            hi = o[half + j * Q_BLOCK:half + (j + 1) * Q_BLOCK, :]
            o_ref[block_rows(r), j * LANES:(j + 1) * LANES] = jnp.where(is_lo, lo, hi).astype(o_ref.dtype)


def _swa_attention(dqkv, band_bias, own, sinks, layer, *, batch, seq, nblk):
    nblk = min(nblk, seq // Q_BLOCK)
    rows = nblk * Q_BLOCK
    qkv = dqkv.reshape(batch, seq, D_WIDTH + 2 * D_KV_WIDTH)
    k_blk = D_WIDTH // LANES
    v_blk = k_blk + 1
    prev = lambda t: jnp.maximum(t * nblk - 1, 0)
    return pl.pallas_call(
        functools.partial(_swa_body, nblk=nblk),
        grid=(batch, seq // rows),
        in_specs=[pl.BlockSpec((None, rows, D_WIDTH), lambda b, t: (b, t, 0)),
                  pl.BlockSpec((None, Q_BLOCK, LANES), lambda b, t: (b, prev(t), k_blk)),
                  pl.BlockSpec((None, rows, LANES), lambda b, t: (b, t, k_blk)),
                  pl.BlockSpec((None, Q_BLOCK, LANES), lambda b, t: (b, prev(t), v_blk)),
                  pl.BlockSpec((None, rows, LANES), lambda b, t: (b, t, v_blk)),
                  _const_spec((2, D_Q_HEADS * Q_BLOCK, Q_BLOCK), (0, 0, 0)),
                  _const_spec((D_Q_HEADS * Q_BLOCK, Q_BLOCK), (0, 0)),
                  _const_spec((None, D_Q_HEADS, LANES), (layer, 0, 0))],
        out_specs=pl.BlockSpec((None, rows, D_WIDTH), lambda b, t: (b, t, 0)),
        out_shape=jax.ShapeDtypeStruct((batch, seq, D_WIDTH), BF16),
        compiler_params=_params(2),
        name="swa_attention",
    )(qkv, qkv, qkv, qkv, qkv, band_bias, own, sinks)


def _merge_body(h_ref, ya_ref, yb_ref, yc_ref, yd_ref,
                gpre_ref, wg_ref, bg_ref, wa_ref, wb_ref, wc_ref, wd_ref, wo_ref, gpost_ref, o_ref):
    tm, d = h_ref.shape
    halves = [slice(i * tm // FFN_SPLIT, (i + 1) * tm // FFN_SPLIT) for i in range(FFN_SPLIT)]
    xs = [h_ref[r, :] for r in halves]
    us = [_rms(x, gpre_ref[...]).astype(BF16) for x in xs]
    branches = ((ya_ref, wa_ref), (yb_ref, wb_ref), (yc_ref, wc_ref), (yd_ref, wd_ref))
    merged = [jnp.zeros((tm // FFN_SPLIT, d), F32) for _ in halves]
    for i, (y_ref, w_ref) in enumerate(branches):
        for j, r in enumerate(halves):
            gate = jax.nn.sigmoid(_dot(us[j], wg_ref[:, i * d:(i + 1) * d]) + bg_ref[:, i * d:(i + 1) * d])
            merged[j] = merged[j] + gate * _dot(y_ref[r, :], w_ref[...])
    outs = [_dot(m.astype(BF16), wo_ref[...]) for m in merged]
    for r, x, out in zip(halves, xs, outs):
        o_ref[r, :] = x + _rms(out, gpost_ref[...])


def _merge(h, ya, yb, yc, yd, layer, gpre, wg, bg, wa, wb, wc, wd, wo, gpost, *, tm):
    n, d = h.shape
    tm = min(tm, n)
    row = lambda t: (t, 0)
    lay = lambda *shape: _const_spec((None,) + shape, (layer,) + (0,) * len(shape))
    return pl.pallas_call(
        _merge_body,
        grid=(n // tm,),
        in_specs=[pl.BlockSpec((tm, d), row),
                  pl.BlockSpec((tm, A_WIDTH), row), pl.BlockSpec((tm, B_WIDTH), row),
                  pl.BlockSpec((tm, C_WIDTH), row), pl.BlockSpec((tm, D_WIDTH), row),
                  lay(1, d), lay(d, N_BRANCH * d), lay(1, N_BRANCH * d),
                  lay(A_WIDTH, d), lay(B_WIDTH, d), lay(C_WIDTH, d), lay(D_WIDTH, d),
                  lay(d, d), lay(1, d)],
        out_specs=pl.BlockSpec((tm, d), row),
        out_shape=jax.ShapeDtypeStruct((n, d), F32),
        compiler_params=_params(1),
        name="merge",
    )(h, ya.reshape(n, A_WIDTH), yb, yc, yd.reshape(n, D_WIDTH),
      gpre, wg, bg, wa, wb, wc, wd, wo, gpost)


def kernel(x, p, ffn1_norm_pre, ffn1_w_gu, ffn1_w_down, ffn1_norm_post, mix_norm_pre, w_in, b_forget, b_gate, conv_short, conv_dw, conv_dw_bias, conv_ln_gain, conv_ln_bias, attn_sinks, rel_bias, w_br_a, w_br_b, w_br_c, w_br_d, w_o, mix_norm_post, ffn2_norm_pre, ffn2_w_gu, ffn2_w_down, ffn2_norm_post, ple_norm_gate, w_ple_gate, w_ple, ple_norm_post):
    batch, seq, d = x.shape
    depth = w_in.shape[0]
    n = batch * seq
    bf = lambda w: w.astype(BF16)
    vec = lambda g: g.astype(F32)[:, None, :]

    scale = HEAD_DIM ** -0.5
    dq = w_in[..., C_END:C_END + D_WIDTH].reshape(depth, d, D_Q_HEADS, HEAD_DIM)
    dq = dq[:, :, D_HEAD_PERM, :].reshape(depth, d, D_WIDTH) * scale
    w_pack = bf(jnp.concatenate(
        [w_in[..., :A_WIDTH] * scale, w_in[..., A_WIDTH:A_QKV_END], w_in[..., A_F_END:C_END], dq,
         w_in[..., C_END + D_WIDTH:D_END],
         jnp.pad(w_in[..., A_QKV_END:A_F_END], ((0, 0), (0, 0), (0, LANES - A_HEADS)))], axis=-1))
    w_gate = bf(w_in[..., D_END:])
    b_forget_row = jnp.pad(b_forget.astype(F32), ((0, 0), (0, LANES - A_HEADS)))[:, None, :]
    w_br_d_perm = bf(w_br_d.reshape(depth, D_Q_HEADS, HEAD_DIM, d)[:, D_HEAD_PERM].reshape(depth, D_WIDTH, d))
    sinks = jnp.broadcast_to(attn_sinks.astype(F32)[:, :, None], (depth, D_Q_HEADS, LANES))
    band_bias, band_own = _band_bias(rel_bias.astype(F32))

    ffn1_w_gu, ffn1_w_down = bf(ffn1_w_gu), bf(ffn1_w_down)
    ffn2_w_gu, ffn2_w_down = bf(ffn2_w_gu), bf(ffn2_w_down)
    w_br_a, w_br_b, w_br_c, w_o = bf(w_br_a), bf(w_br_b), bf(w_br_c), bf(w_o)
    w_ple_gate, w_ple = bf(w_ple_gate), bf(w_ple)
    p2 = p.reshape(depth, n, p.shape[-1])
    conv_taps = jnp.pad(conv_dw.astype(F32), ((0, 0), (0, CONV_TAP_ROWS - CONF_CONV), (0, 0)))

    h = x.reshape(n, d)
    for i in range(depth):
        h = _ffn(h, i, vec(ffn1_norm_pre), ffn1_w_gu, ffn1_w_down, vec(ffn1_norm_post), tm=FFN_TM)
        aqkv, yb, yc, dqkv, logf = _inproj(
            h, i, vec(mix_norm_pre), w_pack, b_forget_row, conv_short.astype(F32),
            conv_taps, vec(conv_dw_bias), vec(conv_ln_gain), vec(conv_ln_bias),
            batch=batch, seq=seq, tm=512)
        ya = _fox_attention(aqkv, logf, batch=batch, seq=seq, tq=512)
        yd = _swa_attention(dqkv, band_bias, band_own, sinks, i, batch=batch, seq=seq, nblk=4)
        h = _merge(h, ya, yb, yc, yd, i, vec(mix_norm_pre), w_gate, vec(b_gate),
                   w_br_a, w_br_b, w_br_c, w_br_d_perm, w_o, vec(mix_norm_post), tm=FFN_TM)
        h = _ffn(h, i, vec(ffn2_norm_pre), ffn2_w_gu, ffn2_w_down, vec(ffn2_norm_post),
                 ple=(p2, vec(ple_norm_gate), w_ple_gate, w_ple, vec(ple_norm_post)), tm=FFN_TM)
    return h.reshape(batch, seq, d)
```

```python
import functools
import math

import jax
import jax.numpy as jnp
from jax import lax
from jax.experimental import pallas as pl
from jax.experimental.pallas import tpu as pltpu

F32 = jnp.float32
BF16 = jnp.bfloat16

HEAD_DIM = 64
A_HEADS = 4
A_WIDTH = A_HEADS * HEAD_DIM
B_WIDTH = 256
SHORT_CONV = 3
C_WIDTH = 256
CONF_CONV = 31
D_Q_HEADS = 8
D_KV_HEADS = 2
D_WIDTH = D_Q_HEADS * HEAD_DIM
D_KV_WIDTH = D_KV_HEADS * HEAD_DIM
WINDOW = 128
Q_BLOCK = 128
N_BRANCH = 4
REL_BUCKETS = 32
REL_MAX_DIST = 128
EPS = 1e-6
NEG_INF = -1e30

LANES = 128
SUBLANES = 8
VMEM_LIMIT = 56 * 1024 * 1024
MXU_WIDTH = 256
FFN_TM = 1024
FFN_CHUNK = MXU_WIDTH
FFN_SPLIT = 4

A_QKV_END = 3 * A_WIDTH
A_F_END = A_QKV_END + A_HEADS
B_END = A_F_END + 3 * B_WIDTH
C_END = B_END + 2 * C_WIDTH
D_END = C_END + D_WIDTH + 2 * D_KV_WIDTH

D_HEAD_PERM = (0, 4, 1, 5, 2, 6, 3, 7)

P_A = 0
P_B = P_A + 3 * A_WIDTH
P_C = P_B + 3 * B_WIDTH
P_D = P_C + 2 * C_WIDTH
P_F = P_D + D_WIDTH + 2 * D_KV_WIDTH
P_END = P_F + LANES


def _rms(x, g):
    return x * lax.rsqrt(jnp.mean(x * x, axis=-1, keepdims=True) + EPS) * g


def _dot(a, b):
    return jnp.dot(a, b, preferred_element_type=F32)


def _dot_nt(a, b):
    return lax.dot_general(a, b, (((1,), (1,)), ((), ())), preferred_element_type=F32)


def _zero_after(x):
    bits = lax.bitcast_convert_type(x, jnp.uint32)
    sixteen = jnp.uint32(16)
    bits = lax.shift_right_logical(lax.shift_right_logical(bits, sixteen), sixteen)
    return lax.bitcast_convert_type(bits, F32)


def _const_spec(shape, index):
    return pl.BlockSpec(shape, lambda *_: index, pipeline_mode=pl.Buffered(1))


def _params(n_axes):
    return pltpu.CompilerParams(dimension_semantics=("arbitrary",) * n_axes,
                                vmem_limit_bytes=VMEM_LIMIT)


def _ffn_body(*refs, d_ff, n_chunks, has_ple):
    if has_ple:
        (h_ref, p_ref, gpre_ref, wgu_ref, wdn_ref, gpost_ref,
         ggate_ref, wpg_ref, wple_ref, gple_ref, o_ref, act_ref) = refs
    else:
        h_ref, gpre_ref, wgu_ref, wdn_ref, gpost_ref, o_ref, act_ref = refs
    tm = h_ref.shape[0]
    halves = [slice(i * tm // FFN_SPLIT, (i + 1) * tm // FFN_SPLIT) for i in range(FFN_SPLIT)]
    xs = [h_ref[r, :] for r in halves]
    us = [_rms(x, gpre_ref[...]).astype(BF16) for x in xs]
    fc = d_ff // n_chunks
    for c in range(n_chunks):
        for r, u in zip(halves, us):
            gate = _dot(u, wgu_ref[:, c * fc:(c + 1) * fc])
            up = _dot(u, wgu_ref[:, d_ff + c * fc:d_ff + (c + 1) * fc])
            act_ref[r, c * fc:(c + 1) * fc] = (gate * jax.nn.sigmoid(gate) * up).astype(BF16)
    fs = [_dot(act_ref[r, :], wdn_ref[...]) for r in halves]
    ys = [x + 0.5 * _rms(f, gpost_ref[...]) for x, f in zip(xs, fs)]
    if has_ple:
        pes = [_dot(p_ref[r, :].astype(BF16), wple_ref[...]) for r in halves]
        pgs = [jax.nn.sigmoid(_dot(_rms(y, ggate_ref[...]).astype(BF16), wpg_ref[...])) for y in ys]
        ys = [y + pg * _rms(pe, gple_ref[...]) for y, pg, pe in zip(ys, pgs, pes)]
    for r, y in zip(halves, ys):
        o_ref[r, :] = y


def _ffn(h, layer, gpre, wgu, wdn, gpost, ple=None, *, tm):
    n, d = h.shape
    d_ff = wdn.shape[1]
    tm = min(tm, n)
    row = lambda t: (t, 0)
    vec = _const_spec((None, 1, d), (layer, 0, 0))
    in_specs = [pl.BlockSpec((tm, d), row)]
    args = [h]
    if ple is not None:
        p, ggate, wpg, wple, gple = ple
        in_specs.append(pl.BlockSpec((None, tm, p.shape[-1]), lambda t: (layer, t, 0)))
        args.append(p)
    in_specs += [vec, _const_spec((None, d, 2 * d_ff), (layer, 0, 0)),
                 _const_spec((None, d_ff, d), (layer, 0, 0)), vec]
    args += [gpre, wgu, wdn, gpost]
    if ple is not None:
        in_specs += [vec, _const_spec((None, d, d), (layer, 0, 0)),
                     _const_spec((None, p.shape[-1], d), (layer, 0, 0)), vec]
        args += [ggate, wpg, wple, gple]
    return pl.pallas_call(
        functools.partial(_ffn_body, d_ff=d_ff, n_chunks=d_ff // FFN_CHUNK, has_ple=ple is not None),
        grid=(n // tm,),
        in_specs=in_specs,
        out_specs=pl.BlockSpec((tm, d), row),
        out_shape=jax.ShapeDtypeStruct((n, d), F32),
        scratch_shapes=[pltpu.VMEM((tm, d_ff), BF16)],
        compiler_params=_params(1),
        name="ffn_ple" if ple is not None else "ffn",
    )(*args)


CONV_HALO = 32
SHORT_HALO = SUBLANES
CONV_ROWS = 64
CONV_SPLIT = 1
CONV_TAP_ROWS = 32


def _inproj_body(h_ref, g_ref, w_ref, bf_ref, cs_ref, cdw_ref, cdb_ref, lng_ref, lnb_ref,
                 aqkv_ref, yb_ref, yc_ref, dqkv_ref, logf_ref,
                 u_ref, zext_ref, gext_ref, gsh_ref, taps_ref):
    tm = h_ref.shape[0]

    @pl.when(pl.program_id(1) == 0)
    def _():
        zext_ref[:SHORT_HALO, :] = jnp.zeros((SHORT_HALO, B_WIDTH), F32)
        gext_ref[:CONV_HALO, :] = jnp.zeros((CONV_HALO, C_WIDTH), F32)

    u_ref[...] = _rms(h_ref[...], g_ref[...]).astype(BF16)

    c = _dot(u_ref[...], w_ref[:, P_C:P_D])
    gext_ref[CONV_HALO:, :] = c[:, :C_WIDTH] * jax.nn.sigmoid(c[:, C_WIDTH:])
    sh_rows = tm + CONV_HALO - SUBLANES
    for s in range(1, SUBLANES):
        gsh_ref[s - 1] = gext_ref[s:s + sh_rows, :]

    def conv_chunk(r0, gate):
        sub = CONV_ROWS // CONV_SPLIT
        for j in range(CONV_SPLIT):
            g_row = (j + 1) * tm // CONV_SPLIT
            zero = _zero_after(gate[g_row - SUBLANES:g_row, :LANES])
            taps_ref[...] = cdw_ref[...] + jnp.tile(zero, (CONV_TAP_ROWS // SUBLANES, C_WIDTH // LANES))
            base = r0 + j * sub
            acc = jnp.broadcast_to(cdb_ref[...], (sub, C_WIDTH))
            for k in range(CONF_CONV):
                off = CONV_HALO - (CONF_CONV - 1) + k
                s = off % SUBLANES
                lo = base + off - s
                win = gext_ref[lo:lo + sub, :] if s == 0 else gsh_ref[s - 1, lo:lo + sub, :]
                acc = acc + taps_ref[k:k + 1, :] * win
            mu = jnp.mean(acc, axis=-1, keepdims=True)
            xc = acc - mu
            y = xc * lax.rsqrt(jnp.mean(xc * xc, axis=-1, keepdims=True) + EPS)
            y = y * lng_ref[...] + lnb_ref[...]
            yc_ref[base:base + sub, :] = (y * jax.nn.sigmoid(y)).astype(BF16)

    def slab(lo):
        return _dot(u_ref[...], w_ref[:, lo:lo + MXU_WIDTH])

    def proj_a(part):
        a = slab(P_A + part * A_WIDTH)
        aqkv_ref[:, part * A_WIDTH:(part + 1) * A_WIDTH] = a.astype(BF16)
        return a

    held = {}

    def proj_b(part):
        b = slab(P_B + part * B_WIDTH)
        if part < 2:
            held[part] = b
            return b
        zext_ref[SHORT_HALO:, :] = held[1] * b
        conv = jnp.zeros((tm, B_WIDTH), F32)
        for k in range(SHORT_CONV):
            off = SHORT_HALO - (SHORT_CONV - 1) + k
            conv = conv + cs_ref[k:k + 1, :] * zext_ref[off:off + tm, :]
        yb_ref[...] = (held[0] * conv).astype(BF16)
        zext_ref[:SHORT_HALO, :] = zext_ref[tm:tm + SHORT_HALO, :]
        return b

    def proj_d(part):
        dd = slab(P_D + part * MXU_WIDTH)
        dqkv_ref[:, part * MXU_WIDTH:(part + 1) * MXU_WIDTH] = dd.astype(BF16)
        return dd

    def proj_f():
        af = _dot(u_ref[...], w_ref[:, P_F:P_END]) + bf_ref[...]
        t = af.T[:SUBLANES, :]
        logf_ref[...] = jnp.minimum(t, 0.0) - jnp.log1p(jnp.exp(-jnp.abs(t)))
        return af

    n_d = (D_WIDTH + 2 * D_KV_WIDTH) // MXU_WIDTH
    projections = ([functools.partial(proj_a, i) for i in range(3)]
                   + [functools.partial(proj_b, i) for i in range(3)]
                   + [functools.partial(proj_d, i) for i in range(n_d)] + [proj_f])
    chunks = list(range(0, tm, CONV_ROWS))
    assert len(chunks) <= len(projections)
    for i, proj in enumerate(projections):
        res = proj()
        if i < len(chunks):
            conv_chunk(chunks[i], res)
    gext_ref[:CONV_HALO, :] = gext_ref[tm:tm + CONV_HALO, :]


def _inproj(h, layer, g, w, bf, cs, cdw, cdb, lng, lnb, *, batch, seq, tm):
    n, d = h.shape
    tm = min(tm, seq)
    nt = seq // tm
    row = lambda b, t: (b * nt + t, 0)
    lay = lambda *shape: _const_spec((None,) + shape, (layer,) + (0,) * len(shape))
    out_shape = (jax.ShapeDtypeStruct((n, 3 * A_WIDTH), BF16),
                 jax.ShapeDtypeStruct((n, B_WIDTH), BF16),
                 jax.ShapeDtypeStruct((n, C_WIDTH), BF16),
                 jax.ShapeDtypeStruct((n, D_WIDTH + 2 * D_KV_WIDTH), BF16),
                 jax.ShapeDtypeStruct((batch, SUBLANES, seq), F32))
    out_specs = (pl.BlockSpec((tm, 3 * A_WIDTH), row),
                 pl.BlockSpec((tm, B_WIDTH), row),
                 pl.BlockSpec((tm, C_WIDTH), row),
                 pl.BlockSpec((tm, D_WIDTH + 2 * D_KV_WIDTH), row),
                 pl.BlockSpec((None, SUBLANES, tm), lambda b, t: (b, 0, t)))
    return pl.pallas_call(
        _inproj_body,
        grid=(batch, nt),
        in_specs=[pl.BlockSpec((tm, d), row), lay(1, d), lay(d, P_END), lay(1, LANES),
                  lay(SHORT_CONV, B_WIDTH), lay(CONV_TAP_ROWS, C_WIDTH), lay(1, C_WIDTH),
                  lay(1, C_WIDTH), lay(1, C_WIDTH)],
        out_specs=out_specs,
        out_shape=out_shape,
        scratch_shapes=[pltpu.VMEM((tm, d), BF16),
                        pltpu.VMEM((tm + SHORT_HALO, B_WIDTH), F32),
                        pltpu.VMEM((tm + CONV_HALO, C_WIDTH), F32),
                        pltpu.VMEM((SUBLANES - 1, tm + CONV_HALO - SUBLANES, C_WIDTH), F32),
                        pltpu.VMEM((CONV_TAP_ROWS, C_WIDTH), F32)],
        compiler_params=_params(2),
        name="inproj",
    )(h, g, w, bf, cs, cdw, cdb, lng, lnb)


def _split3(x):
    hi = x.astype(BF16)
    r = x - hi.astype(F32)
    mid = r.astype(BF16)
    lo = (r - mid.astype(F32)).astype(BF16)
    return hi, mid, lo


def _cumsum_lanes(x):
    rows, s = x.shape
    nb = s // LANES
    stacked = jnp.concatenate([x[:, b * LANES:(b + 1) * LANES] for b in range(nb)], axis=0)
    r = lax.broadcasted_iota(jnp.int32, (LANES, LANES), 0)
    c = lax.broadcasted_iota(jnp.int32, (LANES, LANES), 1)
    upper = jnp.where(r <= c, 1.0, 0.0).astype(BF16)
    hi, mid, lo = _split3(stacked)
    within = (_dot(lo, upper) + _dot(mid, upper)) + _dot(hi, upper)
    tot = jnp.broadcast_to(within[:, LANES - 1:], within.shape)
    n = rows * nb
    rr = lax.broadcasted_iota(jnp.int32, (n, n), 0)
    cc = lax.broadcasted_iota(jnp.int32, (n, n), 1)
    shift = rows.bit_length() - 1
    same_row = (rr & (rows - 1)) == (cc & (rows - 1))
    earlier = lax.shift_right_logical(cc, shift) < lax.shift_right_logical(rr, shift)
    prev = jnp.where(same_row & earlier, 1.0, 0.0).astype(BF16)
    hi, mid, lo = _split3(tot)
    total = within + ((_dot(prev, lo) + _dot(prev, mid)) + _dot(prev, hi))
    return [total[b * rows:(b + 1) * rows, :] for b in range(nb)]


def _fox_body(q_ref, k_ref, v_ref, logf_ref, o_ref, c_ref, *, tq, nq):
    pair = pl.program_id(1)
    qi = pl.program_id(2)
    per_blk = tq // LANES

    @pl.when(qi == 0)
    def _():
        blocks = _cumsum_lanes(logf_ref[...])
        for b, blk in enumerate(blocks):
            lanes = slice((b % per_blk) * LANES, (b % per_blk + 1) * LANES)
            c_ref[b // per_blk, :, lanes] = blk
            c_ref[b // per_blk, 0:2, lanes] = jnp.where(pair == 0, blk[0:2, :], blk[2:4, :])

    q = q_ref[...]
    lane = lax.broadcasted_iota(jnp.int32, (1, LANES), 1)
    is_lo = lane < HEAD_DIM
    zero = jnp.zeros_like(q)
    q_heads = (jnp.where(is_lo, q, zero), jnp.where(is_lo, zero, q))
    c_here = c_ref[qi]
    c_end = (c_here[0:1, tq - 1:tq], c_here[1:2, tq - 1:tq])
    row_id = lax.broadcasted_iota(jnp.int32, (tq, tq), 0)
    col_id = lax.broadcasted_iota(jnp.int32, (tq, tq), 1)

    ones = jnp.ones((tq, LANES), BF16)

    def step(kj, carry, masked):
        rows = slice(kj * tq, (kj + 1) * tq)
        k = k_ref[rows, :]
        vv = jnp.concatenate([v_ref[rows, :], ones], axis=1)
        cb = c_ref[kj]
        scores = []
        for hh in range(2):
            s = _dot_nt(q_heads[hh], k) + (c_end[hh] - cb[hh:hh + 1, :])
            if masked:
                s = jnp.where(row_id >= col_id, s, NEG_INF)
            scores.append(s)
        weights, alphas, maxes = [], [], []
        for hh in range(2):
            m = carry[hh][0]
            m_new = jnp.maximum(m, jnp.max(scores[hh], axis=-1, keepdims=True))
            weights.append(jnp.exp(scores[hh] - m_new).astype(BF16))
            alphas.append(jnp.exp(m - m_new))
            maxes.append(m_new)
        return tuple((maxes[hh], alphas[hh] * carry[hh][1] + _dot(weights[hh], vv))
                     for hh in range(2))

    init_one = (jnp.full((tq, 1), NEG_INF, F32), jnp.zeros((tq, 2 * LANES), F32))

    for n_before in range(nq):
        @pl.when(qi == n_before)
        def _(n_before=n_before):
            carry = (init_one, init_one)
            for kj in range(n_before):
                carry = step(kj, carry, False)
            (_, a0), (_, a1) = step(n_before, carry, True)
            o_ref[...] = jnp.where(is_lo, a0[:, :LANES] * (1.0 / a0[:, LANES:]),
                                   a1[:, :LANES] * (1.0 / a1[:, LANES:])).astype(o_ref.dtype)


def _fox_attention(aqkv, logf, *, batch, seq, tq):
    tq = min(tq, seq)
    nq = seq // tq
    n_pairs = A_WIDTH // LANES
    qkv = aqkv.reshape(batch, seq, 3 * A_WIDTH)
    return pl.pallas_call(
        functools.partial(_fox_body, tq=tq, nq=nq),
        grid=(batch, n_pairs, nq),
        in_specs=[pl.BlockSpec((None, tq, LANES), lambda b, p, i: (b, i, p)),
                  pl.BlockSpec((None, seq, LANES), lambda b, p, i: (b, 0, n_pairs + p)),
                  pl.BlockSpec((None, seq, LANES), lambda b, p, i: (b, 0, 2 * n_pairs + p)),
                  pl.BlockSpec((None, SUBLANES, seq), lambda b, p, i: (b, 0, 0))],
        out_specs=pl.BlockSpec((None, tq, LANES), lambda b, p, i: (b, i, p)),
        out_shape=jax.ShapeDtypeStruct((batch, seq, A_WIDTH), BF16),
        scratch_shapes=[pltpu.VMEM((nq, SUBLANES, tq), F32)],
        compiler_params=_params(3),
        name="fox_attention",
    )(qkv, qkv, qkv, logf)


def _t5_causal_bucket(dist):
    max_exact = REL_BUCKETS // 2
    large = max_exact + (jnp.log(jnp.maximum(dist, 1).astype(F32) / max_exact)
                         / math.log(REL_MAX_DIST / max_exact)
                         * (REL_BUCKETS - max_exact)).astype(jnp.int32)
    large = jnp.minimum(large, REL_BUCKETS - 1)
    return jnp.where(dist < max_exact, dist, large)


def _band_bias_body(bucket_ref, rel_ref, o_ref):
    for part in range(2):
        bucket = bucket_ref[part]
        for h in range(D_Q_HEADS):
            acc = jnp.full(bucket.shape, NEG_INF, F32)
            for b in range(REL_BUCKETS):
                acc = jnp.where(bucket == b, rel_ref[b, h], acc)
            o_ref[part, h] = acc


def _band_bias(rel_bias):
    dist = jnp.maximum(jnp.arange(Q_BLOCK)[:, None] + Q_BLOCK - jnp.arange(2 * Q_BLOCK)[None, :], 0)
    bucket = _t5_causal_bucket(dist).astype(jnp.int32)
    own = jnp.arange(Q_BLOCK)[None, :] <= jnp.arange(Q_BLOCK)[:, None]
    bucket = jnp.stack([jnp.where(own, bucket[:, Q_BLOCK:], -1),
                        jnp.where(own, -1, bucket[:, :Q_BLOCK])])
    bias = pl.pallas_call(
        _band_bias_body,
        in_specs=[pl.BlockSpec(memory_space=pltpu.VMEM), pl.BlockSpec(memory_space=pltpu.SMEM)],
        out_specs=pl.BlockSpec(memory_space=pltpu.VMEM),
        out_shape=jax.ShapeDtypeStruct((2, D_Q_HEADS, Q_BLOCK, Q_BLOCK), F32),
        name="band_bias",
    )(bucket, rel_bias)
    own_all = jnp.tile(own, (D_Q_HEADS, 1)).astype(BF16)
    return bias.reshape(2, D_Q_HEADS * Q_BLOCK, Q_BLOCK), own_all


def _swa_body(q_ref, kp_ref, kc_ref, vp_ref, vc_ref, bias_ref, own_ref, sink_ref, o_ref, *, nblk):
    step = pl.program_id(1)
    n_lane_blk = D_Q_HEADS // 2
    is_lo = lax.broadcasted_iota(jnp.int32, (1, LANES), 1) < HEAD_DIM
    sink = jnp.concatenate([jnp.broadcast_to(sink_ref[h:h + 1, :], (Q_BLOCK, LANES))
                            for h in range(D_Q_HEADS)], axis=0)
    ones = jnp.ones((Q_BLOCK, LANES), BF16)
    no_prev = jnp.where(step > 0, 0.0, NEG_INF)
    def block_rows(r):
        return slice(r * Q_BLOCK, (r + 1) * Q_BLOCK)

    scores = []
    for r in range(nblk):
        rows = block_rows(r)
        q_blks = [q_ref[rows, j * LANES:(j + 1) * LANES] for j in range(n_lane_blk)]
        zero = jnp.zeros_like(q_blks[0])
        qq = jnp.concatenate([jnp.where(is_lo, q, zero) for q in q_blks]
                             + [jnp.where(is_lo, zero, q) for q in q_blks], axis=0)
        kp = kp_ref[...] if r == 0 else kc_ref[block_rows(r - 1), :]
        scores.append(_dot_nt(qq, jnp.concatenate([kc_ref[rows, :], kp], axis=0)))
    weights, maxes = [], []
    for r in range(nblk):
        s2 = scores[r]
        s_own = s2[:, :Q_BLOCK] + bias_ref[0]
        s_prev = s2[:, Q_BLOCK:] + bias_ref[1]
        if r == 0:
            s_prev = s_prev + no_prev
        s = jnp.maximum(s_own, s_prev)
        m = jnp.max(jnp.maximum(s, sink), axis=-1, keepdims=True)
        e = jnp.exp(s - m).astype(BF16)
        e_own = e * own_ref[...]
        weights.append(jnp.concatenate([e_own, e - e_own], axis=1))
        maxes.append(m)
    outs = []
    for r in range(nblk):
        rows = block_rows(r)
        vp = vp_ref[...] if r == 0 else vc_ref[block_rows(r - 1), :]
        vv = jnp.concatenate([jnp.concatenate([vc_ref[rows, :], ones], axis=1),
                              jnp.concatenate([vp, ones], axis=1)], axis=0)
        outs.append(_dot(weights[r], vv))
    half = n_lane_blk * Q_BLOCK
    for r in range(nblk):
        o2 = outs[r]
        o = o2[:, :LANES] * (1.0 / (o2[:, LANES:] + jnp.exp(sink - maxes[r])))
        for j in range(n_lane_blk):
            lo = o[j * Q_BLOCK:(j + 1) * Q_BLOCK, :]
            hi = o[half + j * Q_BLOCK:half + (j + 1) * Q_BLOCK, :]
            o_ref[block_rows(r), j * LANES:(j + 1) * LANES] = jnp.where(is_lo, lo, hi).astype(o_ref.dtype)


def _swa_attention(dqkv, band_bias, own, sinks, layer, *, batch, seq, nblk):
    nblk = min(nblk, seq // Q_BLOCK)
    rows = nblk * Q_BLOCK
    qkv = dqkv.reshape(batch, seq, D_WIDTH + 2 * D_KV_WIDTH)
    k_blk = D_WIDTH // LANES
    v_blk = k_blk + 1
    prev = lambda t: jnp.maximum(t * nblk - 1, 0)
    return pl.pallas_call(
        functools.partial(_swa_body, nblk=nblk),
        grid=(batch, seq // rows),
        in_specs=[pl.BlockSpec((None, rows, D_WIDTH), lambda b, t: (b, t, 0)),
                  pl.BlockSpec((None, Q_BLOCK, LANES), lambda b, t: (b, prev(t), k_blk)),
                  pl.BlockSpec((None, rows, LANES), lambda b, t: (b, t, k_blk)),
                  pl.BlockSpec((None, Q_BLOCK, LANES), lambda b, t: (b, prev(t), v_blk)),
                  pl.BlockSpec((None, rows, LANES), lambda b, t: (b, t, v_blk)),
                  _const_spec((2, D_Q_HEADS * Q_BLOCK, Q_BLOCK), (0, 0, 0)),
                  _const_spec((D_Q_HEADS * Q_BLOCK, Q_BLOCK), (0, 0)),
                  _const_spec((None, D_Q_HEADS, LANES), (layer, 0, 0))],
        out_specs=pl.BlockSpec((None, rows, D_WIDTH), lambda b, t: (b, t, 0)),
        out_shape=jax.ShapeDtypeStruct((batch, seq, D_WIDTH), BF16),
        compiler_params=_params(2),
        name="swa_attention",
    )(qkv, qkv, qkv, qkv, qkv, band_bias, own, sinks)


def _merge_body(h_ref, ya_ref, yb_ref, yc_ref, yd_ref,
                gpre_ref, wg_ref, bg_ref, wa_ref, wb_ref, wc_ref, wd_ref, wo_ref, gpost_ref, o_ref):
    tm, d = h_ref.shape
    halves = [slice(i * tm // FFN_SPLIT, (i + 1) * tm // FFN_SPLIT) for i in range(FFN_SPLIT)]
    xs = [h_ref[r, :] for r in halves]
    us = [_rms(x, gpre_ref[...]).astype(BF16) for x in xs]
    branches = ((ya_ref, wa_ref), (yb_ref, wb_ref), (yc_ref, wc_ref), (yd_ref, wd_ref))
    merged = [jnp.zeros((tm // FFN_SPLIT, d), F32) for _ in halves]
    for i, (y_ref, w_ref) in enumerate(branches):
        for j, r in enumerate(halves):
            gate = jax.nn.sigmoid(_dot(us[j], wg_ref[:, i * d:(i + 1) * d]) + bg_ref[:, i * d:(i + 1) * d])
            merged[j] = merged[j] + gate * _dot(y_ref[r, :], w_ref[...])
    outs = [_dot(m.astype(BF16), wo_ref[...]) for m in merged]
    for r, x, out in zip(halves, xs, outs):
        o_ref[r, :] = x + _rms(out, gpost_ref[...])


def _merge(h, ya, yb, yc, yd, layer, gpre, wg, bg, wa, wb, wc, wd, wo, gpost, *, tm):
    n, d = h.shape
    tm = min(tm, n)
    row = lambda t: (t, 0)
    lay = lambda *shape: _const_spec((None,) + shape, (layer,) + (0,) * len(shape))
    return pl.pallas_call(
        _merge_body,
        grid=(n // tm,),
        in_specs=[pl.BlockSpec((tm, d), row),
                  pl.BlockSpec((tm, A_WIDTH), row), pl.BlockSpec((tm, B_WIDTH), row),
                  pl.BlockSpec((tm, C_WIDTH), row), pl.BlockSpec((tm, D_WIDTH), row),
                  lay(1, d), lay(d, N_BRANCH * d), lay(1, N_BRANCH * d),
                  lay(A_WIDTH, d), lay(B_WIDTH, d), lay(C_WIDTH, d), lay(D_WIDTH, d),
                  lay(d, d), lay(1, d)],
        out_specs=pl.BlockSpec((tm, d), row),
        out_shape=jax.ShapeDtypeStruct((n, d), F32),
        compiler_params=_params(1),
        name="merge",
    )(h, ya.reshape(n, A_WIDTH), yb, yc, yd.reshape(n, D_WIDTH),
      gpre, wg, bg, wa, wb, wc, wd, wo, gpost)


def kernel(x, p, ffn1_norm_pre, ffn1_w_gu, ffn1_w_down, ffn1_norm_post, mix_norm_pre, w_in, b_forget, b_gate, conv_short, conv_dw, conv_dw_bias, conv_ln_gain, conv_ln_bias, attn_sinks, rel_bias, w_br_a, w_br_b, w_br_c, w_br_d, w_o, mix_norm_post, ffn2_norm_pre, ffn2_w_gu, ffn2_w_down, ffn2_norm_post, ple_norm_gate, w_ple_gate, w_ple, ple_norm_post):
    batch, seq, d = x.shape
    depth = w_in.shape[0]
    n = batch * seq
    bf = lambda w: w.astype(BF16)
    vec = lambda g: g.astype(F32)[:, None, :]

    scale = HEAD_DIM ** -0.5
    w_in = bf(w_in)
    dq = w_in[..., C_END:C_END + D_WIDTH].reshape(depth, d, D_Q_HEADS, HEAD_DIM)
    dq = dq[:, :, D_HEAD_PERM, :].reshape(depth, d, D_WIDTH) * scale
    w_pack = jnp.concatenate(
        [w_in[..., :A_WIDTH] * scale, w_in[..., A_WIDTH:A_QKV_END], w_in[..., A_F_END:C_END], dq,
         w_in[..., C_END + D_WIDTH:D_END],
         jnp.pad(w_in[..., A_QKV_END:A_F_END], ((0, 0), (0, 0), (0, LANES - A_HEADS)))], axis=-1)
    w_gate = w_in[..., D_END:]
    b_forget_row = jnp.pad(b_forget.astype(F32), ((0, 0), (0, LANES - A_HEADS)))[:, None, :]
    w_br_d_perm = bf(w_br_d.reshape(depth, D_Q_HEADS, HEAD_DIM, d)[:, D_HEAD_PERM].reshape(depth, D_WIDTH, d))
    sinks = jnp.broadcast_to(attn_sinks.astype(F32)[:, :, None], (depth, D_Q_HEADS, LANES))
    band_bias, band_own = _band_bias(rel_bias.astype(F32))

    ffn1_w_gu, ffn1_w_down = bf(ffn1_w_gu), bf(ffn1_w_down)
    ffn2_w_gu, ffn2_w_down = bf(ffn2_w_gu), bf(ffn2_w_down)
    w_br_a, w_br_b, w_br_c, w_o = bf(w_br_a), bf(w_br_b), bf(w_br_c), bf(w_o)
    w_ple_gate, w_ple = bf(w_ple_gate), bf(w_ple)
    p2 = p.reshape(depth, n, p.shape[-1])
    conv_taps = jnp.pad(conv_dw.astype(F32), ((0, 0), (0, CONV_TAP_ROWS - CONF_CONV), (0, 0)))

    h = x.reshape(n, d)
    for i in range(depth):
        h = _ffn(h, i, vec(ffn1_norm_pre), ffn1_w_gu, ffn1_w_down, vec(ffn1_norm_post), tm=FFN_TM)
        aqkv, yb, yc, dqkv, logf = _inproj(
            h, i, vec(mix_norm_pre), w_pack, b_forget_row, conv_short.astype(F32),
            conv_taps, vec(conv_dw_bias), vec(conv_ln_gain), vec(conv_ln_bias),
            batch=batch, seq=seq, tm=512)
        ya = _fox_attention(aqkv, logf, batch=batch, seq=seq, tq=512)
        yd = _swa_attention(dqkv, band_bias, band_own, sinks, i, batch=batch, seq=seq, nblk=4)
        h = _merge(h, ya, yb, yc, yd, i, vec(mix_norm_pre), w_gate, vec(b_gate),
                   w_br_a, w_br_b, w_br_c, w_br_d_perm, w_o, vec(mix_norm_post), tm=FFN_TM)
        h = _ffn(h, i, vec(ffn2_norm_pre), ffn2_w_gu, ffn2_w_down, vec(ffn2_norm_post),
                 ple=(p2, vec(ple_norm_gate), w_ple_gate, w_ple, vec(ple_norm_post)), tm=FFN_TM)
    return h.reshape(batch, seq, d)
```

```python
import functools
import math

import jax
import jax.numpy as jnp
from jax import lax
from jax.experimental import pallas as pl
from jax.experimental.pallas import tpu as pltpu

F32 = jnp.float32
BF16 = jnp.bfloat16

HEAD_DIM = 64
A_HEADS = 4
A_WIDTH = A_HEADS * HEAD_DIM
B_WIDTH = 256
SHORT_CONV = 3
C_WIDTH = 256
CONF_CONV = 31
D_Q_HEADS = 8
D_KV_HEADS = 2
D_WIDTH = D_Q_HEADS * HEAD_DIM
D_KV_WIDTH = D_KV_HEADS * HEAD_DIM
WINDOW = 128
Q_BLOCK = 128
N_BRANCH = 4
REL_BUCKETS = 32
REL_MAX_DIST = 128
EPS = 1e-6
NEG_INF = -1e30
LOG2E = math.log2(math.e)

LANES = 128
SUBLANES = 8
VMEM_LIMIT = 56 * 1024 * 1024
MXU_WIDTH = 256
FFN_TM = 1024
FFN_CHUNK = MXU_WIDTH
FFN_SPLIT = 4

A_QKV_END = 3 * A_WIDTH
A_F_END = A_QKV_END + A_HEADS
B_END = A_F_END + 3 * B_WIDTH
C_END = B_END + 2 * C_WIDTH
D_END = C_END + D_WIDTH + 2 * D_KV_WIDTH

D_HEAD_PERM = (0, 4, 1, 5, 2, 6, 3, 7)

P_A = 0
P_B = P_A + 3 * A_WIDTH
P_C = P_B + 3 * B_WIDTH
P_D = P_C + 2 * C_WIDTH
P_F = P_D + D_WIDTH + 2 * D_KV_WIDTH
P_END = P_F + LANES


def _rms(x, g):
    return x * lax.rsqrt(jnp.mean(x * x, axis=-1, keepdims=True) + EPS) * g


def _dot(a, b):
    return jnp.dot(a, b, preferred_element_type=F32)


def _dot_nt(a, b):
    return lax.dot_general(a, b, (((1,), (1,)), ((), ())), preferred_element_type=F32)


def _zero_after(x):
    bits = lax.bitcast_convert_type(x, jnp.uint32)
    sixteen = jnp.uint32(16)
    bits = lax.shift_right_logical(lax.shift_right_logical(bits, sixteen), sixteen)
    return lax.bitcast_convert_type(bits, F32)


def _const_spec(shape, index):
    return pl.BlockSpec(shape, lambda *_: index, pipeline_mode=pl.Buffered(1))


def _params(n_axes):
    return pltpu.CompilerParams(dimension_semantics=("arbitrary",) * n_axes,
                                vmem_limit_bytes=VMEM_LIMIT)


def _ffn_body(*refs, d_ff, n_chunks, has_ple):
    if has_ple:
        (h_ref, p_ref, gpre_ref, wgu_ref, wdn_ref, gpost_ref,
         ggate_ref, wpg_ref, wple_ref, gple_ref, o_ref, act_ref) = refs
    else:
        h_ref, gpre_ref, wgu_ref, wdn_ref, gpost_ref, o_ref, act_ref = refs
    tm = h_ref.shape[0]
    halves = [slice(i * tm // FFN_SPLIT, (i + 1) * tm // FFN_SPLIT) for i in range(FFN_SPLIT)]
    xs = [h_ref[r, :] for r in halves]
    us = [_rms(x, gpre_ref[...]).astype(BF16) for x in xs]
    fc = d_ff // n_chunks
    for c in range(n_chunks):
        for r, u in zip(halves, us):
            gate = _dot(u, wgu_ref[:, c * fc:(c + 1) * fc])
            up = _dot(u, wgu_ref[:, d_ff + c * fc:d_ff + (c + 1) * fc])
            act_ref[r, c * fc:(c + 1) * fc] = (gate * jax.nn.sigmoid(gate) * up).astype(BF16)
    fs = [_dot(act_ref[r, :], wdn_ref[...]) for r in halves]
    ys = [x + 0.5 * _rms(f, gpost_ref[...]) for x, f in zip(xs, fs)]
    if has_ple:
        pes = [_dot(p_ref[r, :].astype(BF16), wple_ref[...]) for r in halves]
        pgs = [jax.nn.sigmoid(_dot(_rms(y, ggate_ref[...]).astype(BF16), wpg_ref[...])) for y in ys]
        ys = [y + pg * _rms(pe, gple_ref[...]) for y, pg, pe in zip(ys, pgs, pes)]
    for r, y in zip(halves, ys):
        o_ref[r, :] = y


def _ffn(h, layer, gpre, wgu, wdn, gpost, ple=None, *, tm):
    n, d = h.shape
    d_ff = wdn.shape[1]
    tm = min(tm, n)
    row = lambda t: (t, 0)
    vec = _const_spec((None, 1, d), (layer, 0, 0))
    in_specs = [pl.BlockSpec((tm, d), row)]
    args = [h]
    if ple is not None:
        p, ggate, wpg, wple, gple = ple
        in_specs.append(pl.BlockSpec((None, tm, p.shape[-1]), lambda t: (layer, t, 0)))
        args.append(p)
    in_specs += [vec, _const_spec((None, d, 2 * d_ff), (layer, 0, 0)),
                 _const_spec((None, d_ff, d), (layer, 0, 0)), vec]
    args += [gpre, wgu, wdn, gpost]
    if ple is not None:
        in_specs += [vec, _const_spec((None, d, d), (layer, 0, 0)),
                     _const_spec((None, p.shape[-1], d), (layer, 0, 0)), vec]
        args += [ggate, wpg, wple, gple]
    return pl.pallas_call(
        functools.partial(_ffn_body, d_ff=d_ff, n_chunks=d_ff // FFN_CHUNK, has_ple=ple is not None),
        grid=(n // tm,),
        in_specs=in_specs,
        out_specs=pl.BlockSpec((tm, d), row),
        out_shape=jax.ShapeDtypeStruct((n, d), F32),
        scratch_shapes=[pltpu.VMEM((tm, d_ff), BF16)],
        compiler_params=_params(1),
        name="ffn_ple" if ple is not None else "ffn",
    )(*args)


CONV_HALO = 32
SHORT_HALO = SUBLANES
CONV_ROWS = 64
CONV_SPLIT = 1
CONV_TAP_ROWS = 32


def _inproj_body(h_ref, g_ref, w_ref, bf_ref, cs_ref, cdw_ref, cdb_ref, lng_ref, lnb_ref,
                 aqkv_ref, yb_ref, yc_ref, dqkv_ref, logf_ref,
                 u_ref, zext_ref, gext_ref, gsh_ref, taps_ref):
    tm = h_ref.shape[0]

    @pl.when(pl.program_id(1) == 0)
    def _():
        zext_ref[:SHORT_HALO, :] = jnp.zeros((SHORT_HALO, B_WIDTH), F32)
        gext_ref[:CONV_HALO, :] = jnp.zeros((CONV_HALO, C_WIDTH), F32)

    u_ref[...] = _rms(h_ref[...], g_ref[...]).astype(BF16)

    c = _dot(u_ref[...], w_ref[:, P_C:P_D])
    gext_ref[CONV_HALO:, :] = c[:, :C_WIDTH] * jax.nn.sigmoid(c[:, C_WIDTH:])
    sh_rows = tm + CONV_HALO - SUBLANES
    for s in range(1, SUBLANES):
        gsh_ref[s - 1] = gext_ref[s:s + sh_rows, :]

    def conv_chunk(r0, gate):
        sub = CONV_ROWS // CONV_SPLIT
        for j in range(CONV_SPLIT):
            g_row = (j + 1) * tm // CONV_SPLIT
            zero = _zero_after(gate[g_row - SUBLANES:g_row, :LANES])
            taps_ref[...] = cdw_ref[...] + jnp.tile(zero, (CONV_TAP_ROWS // SUBLANES, C_WIDTH // LANES))
            base = r0 + j * sub
            acc = jnp.broadcast_to(cdb_ref[...], (sub, C_WIDTH))
            for k in range(CONF_CONV):
                off = CONV_HALO - (CONF_CONV - 1) + k
                s = off % SUBLANES
                lo = base + off - s
                win = gext_ref[lo:lo + sub, :] if s == 0 else gsh_ref[s - 1, lo:lo + sub, :]
                acc = acc + taps_ref[k:k + 1, :] * win
            mu = jnp.mean(acc, axis=-1, keepdims=True)
            xc = acc - mu
            y = xc * lax.rsqrt(jnp.mean(xc * xc, axis=-1, keepdims=True) + EPS)
            y = y * lng_ref[...] + lnb_ref[...]
            yc_ref[base:base + sub, :] = (y * jax.nn.sigmoid(y)).astype(BF16)

    def slab(lo):
        return _dot(u_ref[...], w_ref[:, lo:lo + MXU_WIDTH])

    def proj_a(part):
        a = slab(P_A + part * A_WIDTH)
        aqkv_ref[:, part * A_WIDTH:(part + 1) * A_WIDTH] = a.astype(BF16)
        return a

    held = {}

    def proj_b(part):
        b = slab(P_B + part * B_WIDTH)
        if part < 2:
            held[part] = b
            return b
        zext_ref[SHORT_HALO:, :] = held[1] * b
        conv = jnp.zeros((tm, B_WIDTH), F32)
        for k in range(SHORT_CONV):
            off = SHORT_HALO - (SHORT_CONV - 1) + k
            conv = conv + cs_ref[k:k + 1, :] * zext_ref[off:off + tm, :]
        yb_ref[...] = (held[0] * conv).astype(BF16)
        zext_ref[:SHORT_HALO, :] = zext_ref[tm:tm + SHORT_HALO, :]
        return b

    def proj_d(part):
        dd = slab(P_D + part * MXU_WIDTH)
        dqkv_ref[:, part * MXU_WIDTH:(part + 1) * MXU_WIDTH] = dd.astype(BF16)
        return dd

    def proj_f():
        af = _dot(u_ref[...], w_ref[:, P_F:P_END]) + bf_ref[...]
        t = af.T[:SUBLANES, :]
        logf_ref[...] = (jnp.minimum(t, 0.0) - jnp.log1p(jnp.exp(-jnp.abs(t)))) * LOG2E
        return af

    n_d = (D_WIDTH + 2 * D_KV_WIDTH) // MXU_WIDTH
    projections = ([functools.partial(proj_a, i) for i in range(3)]
                   + [functools.partial(proj_b, i) for i in range(3)]
                   + [functools.partial(proj_d, i) for i in range(n_d)] + [proj_f])
    chunks = list(range(0, tm, CONV_ROWS))
    assert len(chunks) <= len(projections)
    for i, proj in enumerate(projections):
        res = proj()
        if i < len(chunks):
            conv_chunk(chunks[i], res)
    gext_ref[:CONV_HALO, :] = gext_ref[tm:tm + CONV_HALO, :]


def _inproj(h, layer, g, w, bf, cs, cdw, cdb, lng, lnb, *, batch, seq, tm):
    n, d = h.shape
    tm = min(tm, seq)
    nt = seq // tm
    row = lambda b, t: (b * nt + t, 0)
    lay = lambda *shape: _const_spec((None,) + shape, (layer,) + (0,) * len(shape))
    out_shape = (jax.ShapeDtypeStruct((n, 3 * A_WIDTH), BF16),
                 jax.ShapeDtypeStruct((n, B_WIDTH), BF16),
                 jax.ShapeDtypeStruct((n, C_WIDTH), BF16),
                 jax.ShapeDtypeStruct((n, D_WIDTH + 2 * D_KV_WIDTH), BF16),
                 jax.ShapeDtypeStruct((batch, SUBLANES, seq), F32))
    out_specs = (pl.BlockSpec((tm, 3 * A_WIDTH), row),
                 pl.BlockSpec((tm, B_WIDTH), row),
                 pl.BlockSpec((tm, C_WIDTH), row),
                 pl.BlockSpec((tm, D_WIDTH + 2 * D_KV_WIDTH), row),
                 pl.BlockSpec((None, SUBLANES, tm), lambda b, t: (b, 0, t)))
    return pl.pallas_call(
        _inproj_body,
        grid=(batch, nt),
        in_specs=[pl.BlockSpec((tm, d), row), lay(1, d), lay(d, P_END), lay(1, LANES),
                  lay(SHORT_CONV, B_WIDTH), lay(CONV_TAP_ROWS, C_WIDTH), lay(1, C_WIDTH),
                  lay(1, C_WIDTH), lay(1, C_WIDTH)],
        out_specs=out_specs,
        out_shape=out_shape,
        scratch_shapes=[pltpu.VMEM((tm, d), BF16),
                        pltpu.VMEM((tm + SHORT_HALO, B_WIDTH), F32),
                        pltpu.VMEM((tm + CONV_HALO, C_WIDTH), F32),
                        pltpu.VMEM((SUBLANES - 1, tm + CONV_HALO - SUBLANES, C_WIDTH), F32),
                        pltpu.VMEM((CONV_TAP_ROWS, C_WIDTH), F32)],
        compiler_params=_params(2),
        name="inproj",
    )(h, g, w, bf, cs, cdw, cdb, lng, lnb)


def _split3(x):
    hi = x.astype(BF16)
    r = x - hi.astype(F32)
    mid = r.astype(BF16)
    lo = (r - mid.astype(F32)).astype(BF16)
    return hi, mid, lo


def _cumsum_lanes(x):
    rows, s = x.shape
    nb = s // LANES
    stacked = jnp.concatenate([x[:, b * LANES:(b + 1) * LANES] for b in range(nb)], axis=0)
    r = lax.broadcasted_iota(jnp.int32, (LANES, LANES), 0)
    c = lax.broadcasted_iota(jnp.int32, (LANES, LANES), 1)
    upper = jnp.where(r <= c, 1.0, 0.0).astype(BF16)
    hi, mid, lo = _split3(stacked)
    within = (_dot(lo, upper) + _dot(mid, upper)) + _dot(hi, upper)
    tot = jnp.broadcast_to(within[:, LANES - 1:], within.shape)
    n = rows * nb
    rr = lax.broadcasted_iota(jnp.int32, (n, n), 0)
    cc = lax.broadcasted_iota(jnp.int32, (n, n), 1)
    shift = rows.bit_length() - 1
    same_row = (rr & (rows - 1)) == (cc & (rows - 1))
    earlier = lax.shift_right_logical(cc, shift) < lax.shift_right_logical(rr, shift)
    prev = jnp.where(same_row & earlier, 1.0, 0.0).astype(BF16)
    hi, mid, lo = _split3(tot)
    total = within + ((_dot(prev, lo) + _dot(prev, mid)) + _dot(prev, hi))
    return [total[b * rows:(b + 1) * rows, :] for b in range(nb)]


def _fox_body(q_ref, k_ref, v_ref, logf_ref, o_ref, c_ref, *, tq, nq):
    qi = pl.program_id(1)
    per_blk = tq // LANES
    heads = range(A_HEADS)

    @pl.when(qi == 0)
    def _():
        for b, blk in enumerate(_cumsum_lanes(logf_ref[...])):
            c_ref[b // per_blk, :, (b % per_blk) * LANES:(b % per_blk + 1) * LANES] = blk

    def lane_block(ref, rows, h):
        return ref[rows, (h // 2) * LANES:(h // 2 + 1) * LANES]

    is_lo = lax.broadcasted_iota(jnp.int32, (1, LANES), 1) < HEAD_DIM
    q_heads = []
    for h in heads:
        q = lane_block(q_ref, slice(None), h)
        zero = jnp.zeros_like(q)
        q_heads.append(jnp.where(is_lo, q, zero) if h % 2 == 0 else jnp.where(is_lo, zero, q))
    c_here = c_ref[qi]
    c_end = [c_here[h:h + 1, tq - 1:tq] for h in heads]
    row_id = lax.broadcasted_iota(jnp.int32, (tq, tq), 0)
    col_id = lax.broadcasted_iota(jnp.int32, (tq, tq), 1)
    ones = jnp.ones((tq, LANES), BF16)

    def step(kj, carry, masked):
        rows = slice(kj * tq, (kj + 1) * tq)
        cb = c_ref[kj]
        scores = []
        for h in heads:
            s = _dot_nt(q_heads[h], lane_block(k_ref, rows, h)) + (c_end[h] - cb[h:h + 1, :])
            if masked:
                s = jnp.where(row_id >= col_id, s, NEG_INF)
            scores.append(s)
        weights, alphas, maxes = [], [], []
        for h in heads:
            m = carry[h][0]
            m_new = jnp.maximum(m, jnp.max(scores[h], axis=-1, keepdims=True))
            weights.append(jnp.exp2(scores[h] - m_new).astype(BF16))
            alphas.append(jnp.exp2(m - m_new))
            maxes.append(m_new)
        vvs = [jnp.concatenate([lane_block(v_ref, rows, h), ones], axis=1) for h in heads[::2]]
        return tuple((maxes[h], alphas[h] * carry[h][1] + _dot(weights[h], vvs[h // 2]))
                     for h in heads)

    init_one = (jnp.full((tq, 1), NEG_INF, F32), jnp.zeros((tq, 2 * LANES), F32))

    for n_before in range(nq):
        @pl.when(qi == n_before)
        def _(n_before=n_before):
            carry = (init_one,) * A_HEADS
            for kj in range(n_before):
                carry = step(kj, carry, False)
            acc = [a for _, a in step(n_before, carry, True)]
            out = [a[:, :LANES] * (1.0 / a[:, LANES:]) for a in acc]
            for p in range(A_HEADS // 2):
                o_ref[:, p * LANES:(p + 1) * LANES] = jnp.where(
                    is_lo, out[2 * p], out[2 * p + 1]).astype(o_ref.dtype)


def _fox_attention(aqkv, logf, *, batch, seq, tq):
    tq = min(tq, seq)
    nq = seq // tq
    qkv = aqkv.reshape(batch, seq, 3 * A_WIDTH)
    return pl.pallas_call(
        functools.partial(_fox_body, tq=tq, nq=nq),
        grid=(batch, nq),
        in_specs=[pl.BlockSpec((None, tq, A_WIDTH), lambda b, i: (b, i, 0)),
                  pl.BlockSpec((None, seq, A_WIDTH), lambda b, i: (b, 0, 1)),
                  pl.BlockSpec((None, seq, A_WIDTH), lambda b, i: (b, 0, 2)),
                  pl.BlockSpec((None, SUBLANES, seq), lambda b, i: (b, 0, 0))],
        out_specs=pl.BlockSpec((None, tq, A_WIDTH), lambda b, i: (b, i, 0)),
        out_shape=jax.ShapeDtypeStruct((batch, seq, A_WIDTH), BF16),
        scratch_shapes=[pltpu.VMEM((nq, SUBLANES, tq), F32)],
        compiler_params=_params(2),
        name="fox_attention",
    )(qkv, qkv, qkv, logf)


def _t5_causal_bucket(dist):
    max_exact = REL_BUCKETS // 2
    large = max_exact + (jnp.log(jnp.maximum(dist, 1).astype(F32) / max_exact)
                         / math.log(REL_MAX_DIST / max_exact)
                         * (REL_BUCKETS - max_exact)).astype(jnp.int32)
    large = jnp.minimum(large, REL_BUCKETS - 1)
    return jnp.where(dist < max_exact, dist, large)


def _band_bias_body(bucket_ref, rel_ref, o_ref):
    for part in range(2):
        bucket = bucket_ref[part]
        for h in range(D_Q_HEADS):
            acc = jnp.full(bucket.shape, NEG_INF, F32)
            for b in range(REL_BUCKETS):
                acc = jnp.where(bucket == b, rel_ref[b, h], acc)
            o_ref[part, h] = acc


def _band_bias(rel_bias):
    dist = jnp.maximum(jnp.arange(Q_BLOCK)[:, None] + Q_BLOCK - jnp.arange(2 * Q_BLOCK)[None, :], 0)
    bucket = _t5_causal_bucket(dist).astype(jnp.int32)
    own = jnp.arange(Q_BLOCK)[None, :] <= jnp.arange(Q_BLOCK)[:, None]
    bucket = jnp.stack([jnp.where(own, bucket[:, Q_BLOCK:], -1),
                        jnp.where(own, -1, bucket[:, :Q_BLOCK])])
    bias = pl.pallas_call(
        _band_bias_body,
        in_specs=[pl.BlockSpec(memory_space=pltpu.VMEM), pl.BlockSpec(memory_space=pltpu.SMEM)],
        out_specs=pl.BlockSpec(memory_space=pltpu.VMEM),
        out_shape=jax.ShapeDtypeStruct((2, D_Q_HEADS, Q_BLOCK, Q_BLOCK), F32),
        name="band_bias",
    )(bucket, rel_bias)
    own_all = jnp.tile(own, (D_Q_HEADS, 1)).astype(BF16)
    return bias.reshape(2, D_Q_HEADS * Q_BLOCK, Q_BLOCK), own_all


def _swa_body(q_ref, kp_ref, kc_ref, vp_ref, vc_ref, bias_ref, own_ref, sink_ref, o_ref, *, nblk):
    step = pl.program_id(1)
    n_lane_blk = D_Q_HEADS // 2
    is_lo = lax.broadcasted_iota(jnp.int32, (1, LANES), 1) < HEAD_DIM
    sink = jnp.concatenate([jnp.broadcast_to(sink_ref[h:h + 1, :], (Q_BLOCK, LANES))
                            for h in range(D_Q_HEADS)], axis=0)
    ones = jnp.ones((Q_BLOCK, LANES), BF16)
    no_prev = jnp.where(step > 0, 0.0, NEG_INF)
    def block_rows(r):
        return slice(r * Q_BLOCK, (r + 1) * Q_BLOCK)

    scores = []
    for r in range(nblk):
        rows = block_rows(r)
        q_blks = [q_ref[rows, j * LANES:(j + 1) * LANES] for j in range(n_lane_blk)]
        zero = jnp.zeros_like(q_blks[0])
        qq = jnp.concatenate([jnp.where(is_lo, q, zero) for q in q_blks]
                             + [jnp.where(is_lo, zero, q) for q in q_blks], axis=0)
        kp = kp_ref[...] if r == 0 else kc_ref[block_rows(r - 1), :]
        scores.append(_dot_nt(qq, jnp.concatenate([kc_ref[rows, :], kp], axis=0)))
    weights, maxes = [], []
    for r in range(nblk):
        s2 = scores[r]
        s_own = s2[:, :Q_BLOCK] + bias_ref[0]
        s_prev = s2[:, Q_BLOCK:] + bias_ref[1]
        if r == 0:
            s_prev = s_prev + no_prev
        s = jnp.maximum(s_own, s_prev)
        m = jnp.max(jnp.maximum(s, sink), axis=-1, keepdims=True)
        e = jnp.exp(s - m).astype(BF16)
        e_own = e * own_ref[...]
        weights.append(jnp.concatenate([e_own, e - e_own], axis=1))
        maxes.append(m)
    outs = []
    for r in range(nblk):
        rows = block_rows(r)
        vp = vp_ref[...] if r == 0 else vc_ref[block_rows(r - 1), :]
        vv = jnp.concatenate([jnp.concatenate([vc_ref[rows, :], ones], axis=1),
                              jnp.concatenate([vp, ones], axis=1)], axis=0)
        outs.append(_dot(weights[r], vv))
    half = n_lane_blk * Q_BLOCK
    for r in range(nblk):
        o2 = outs[r]
        o = o2[:, :LANES] * (1.0 / (o2[:, LANES:] + jnp.exp(sink - maxes[r])))
        for j in range(n_lane_blk):
            lo = o[j * Q_BLOCK:(j + 1) * Q_BLOCK, :]
            hi = o[half + j * Q_BLOCK:half + (j + 1) * Q_BLOCK, :]
            o_ref[block_rows(r), j * LANES:(j + 1) * LANES] = jnp.where(is_lo, lo, hi).astype(o_ref.dtype)


def _swa_attention(dqkv, band_bias, own, sinks, layer, *, batch, seq, nblk):
    nblk = min(nblk, seq // Q_BLOCK)
    rows = nblk * Q_BLOCK
    qkv = dqkv.reshape(batch, seq, D_WIDTH + 2 * D_KV_WIDTH)
    k_blk = D_WIDTH // LANES
    v_blk = k_blk + 1
    prev = lambda t: jnp.maximum(t * nblk - 1, 0)
    return pl.pallas_call(
        functools.partial(_swa_body, nblk=nblk),
        grid=(batch, seq // rows),
        in_specs=[pl.BlockSpec((None, rows, D_WIDTH), lambda b, t: (b, t, 0)),
                  pl.BlockSpec((None, Q_BLOCK, LANES), lambda b, t: (b, prev(t), k_blk)),
                  pl.BlockSpec((None, rows, LANES), lambda b, t: (b, t, k_blk)),
                  pl.BlockSpec((None, Q_BLOCK, LANES), lambda b, t: (b, prev(t), v_blk)),
                  pl.BlockSpec((None, rows, LANES), lambda b, t: (b, t, v_blk)),
                  _const_spec((2, D_Q_HEADS * Q_BLOCK, Q_BLOCK), (0, 0, 0)),
                  _const_spec((D_Q_HEADS * Q_BLOCK, Q_BLOCK), (0, 0)),
                  _const_spec((None, D_Q_HEADS, LANES), (layer, 0, 0))],
        out_specs=pl.BlockSpec((None, rows, D_WIDTH), lambda b, t: (b, t, 0)),
        out_shape=jax.ShapeDtypeStruct((batch, seq, D_WIDTH), BF16),
        compiler_params=_params(2),
        name="swa_attention",
    )(qkv, qkv, qkv, qkv, qkv, band_bias, own, sinks)


def _merge_body(h_ref, ya_ref, yb_ref, yc_ref, yd_ref,
                gpre_ref, wg_ref, bg_ref, wa_ref, wb_ref, wc_ref, wd_ref, wo_ref, gpost_ref, o_ref):
    tm, d = h_ref.shape
    halves = [slice(i * tm // FFN_SPLIT, (i + 1) * tm // FFN_SPLIT) for i in range(FFN_SPLIT)]
    xs = [h_ref[r, :] for r in halves]
    us = [_rms(x, gpre_ref[...]).astype(BF16) for x in xs]
    branches = ((ya_ref, wa_ref), (yb_ref, wb_ref), (yc_ref, wc_ref), (yd_ref, wd_ref))
    merged = [jnp.zeros((tm // FFN_SPLIT, d), F32) for _ in halves]
    for i, (y_ref, w_ref) in enumerate(branches):
        for j, r in enumerate(halves):
            gate = jax.nn.sigmoid(_dot(us[j], wg_ref[:, i * d:(i + 1) * d]) + bg_ref[:, i * d:(i + 1) * d])
            merged[j] = merged[j] + gate * _dot(y_ref[r, :], w_ref[...])
    outs = [_dot(m.astype(BF16), wo_ref[...]) for m in merged]
    for r, x, out in zip(halves, xs, outs):
        o_ref[r, :] = x + _rms(out, gpost_ref[...])


def _merge(h, ya, yb, yc, yd, layer, gpre, wg, bg, wa, wb, wc, wd, wo, gpost, *, tm):
    n, d = h.shape
    tm = min(tm, n)
    row = lambda t: (t, 0)
    lay = lambda *shape: _const_spec((None,) + shape, (layer,) + (0,) * len(shape))
    return pl.pallas_call(
        _merge_body,
        grid=(n // tm,),
        in_specs=[pl.BlockSpec((tm, d), row),
                  pl.BlockSpec((tm, A_WIDTH), row), pl.BlockSpec((tm, B_WIDTH), row),
                  pl.BlockSpec((tm, C_WIDTH), row), pl.BlockSpec((tm, D_WIDTH), row),
                  lay(1, d), lay(d, N_BRANCH * d), lay(1, N_BRANCH * d),
                  lay(A_WIDTH, d), lay(B_WIDTH, d), lay(C_WIDTH, d), lay(D_WIDTH, d),
                  lay(d, d), lay(1, d)],
        out_specs=pl.BlockSpec((tm, d), row),
        out_shape=jax.ShapeDtypeStruct((n, d), F32),
        compiler_params=_params(1),
        name="merge",
    )(h, ya.reshape(n, A_WIDTH), yb, yc, yd.reshape(n, D_WIDTH),
      gpre, wg, bg, wa, wb, wc, wd, wo, gpost)


def kernel(x, p, ffn1_norm_pre, ffn1_w_gu, ffn1_w_down, ffn1_norm_post, mix_norm_pre, w_in, b_forget, b_gate, conv_short, conv_dw, conv_dw_bias, conv_ln_gain, conv_ln_bias, attn_sinks, rel_bias, w_br_a, w_br_b, w_br_c, w_br_d, w_o, mix_norm_post, ffn2_norm_pre, ffn2_w_gu, ffn2_w_down, ffn2_norm_post, ple_norm_gate, w_ple_gate, w_ple, ple_norm_post):
    batch, seq, d = x.shape
    depth = w_in.shape[0]
    n = batch * seq
    bf = lambda w: w.astype(BF16)
    vec = lambda g: g.astype(F32)[:, None, :]

    scale = HEAD_DIM ** -0.5
    aq = bf(w_in[..., :A_WIDTH] * (scale * LOG2E))
    w_in = bf(w_in)
    dq = w_in[..., C_END:C_END + D_WIDTH].reshape(depth, d, D_Q_HEADS, HEAD_DIM)
    dq = dq[:, :, D_HEAD_PERM, :].reshape(depth, d, D_WIDTH) * scale
    w_pack = jnp.concatenate(
        [aq, w_in[..., A_WIDTH:A_QKV_END], w_in[..., A_F_END:C_END], dq,
         w_in[..., C_END + D_WIDTH:D_END],
         jnp.pad(w_in[..., A_QKV_END:A_F_END], ((0, 0), (0, 0), (0, LANES - A_HEADS)))], axis=-1)
    w_gate = w_in[..., D_END:]
    b_forget_row = jnp.pad(b_forget.astype(F32), ((0, 0), (0, LANES - A_HEADS)))[:, None, :]
    w_br_d_perm = bf(w_br_d.reshape(depth, D_Q_HEADS, HEAD_DIM, d)[:, D_HEAD_PERM].reshape(depth, D_WIDTH, d))
    sinks = jnp.broadcast_to(attn_sinks.astype(F32)[:, :, None], (depth, D_Q_HEADS, LANES))
    band_bias, band_own = _band_bias(rel_bias.astype(F32))

    ffn1_w_gu, ffn1_w_down = bf(ffn1_w_gu), bf(ffn1_w_down)
    ffn2_w_gu, ffn2_w_down = bf(ffn2_w_gu), bf(ffn2_w_down)
    w_br_a, w_br_b, w_br_c, w_o = bf(w_br_a), bf(w_br_b), bf(w_br_c), bf(w_o)
    w_ple_gate, w_ple = bf(w_ple_gate), bf(w_ple)
    p2 = p.reshape(depth, n, p.shape[-1])
    conv_taps = jnp.pad(conv_dw.astype(F32), ((0, 0), (0, CONV_TAP_ROWS - CONF_CONV), (0, 0)))

    h = x.reshape(n, d)
    for i in range(depth):
        h = _ffn(h, i, vec(ffn1_norm_pre), ffn1_w_gu, ffn1_w_down, vec(ffn1_norm_post), tm=FFN_TM)
        aqkv, yb, yc, dqkv, logf = _inproj(
            h, i, vec(mix_norm_pre), w_pack, b_forget_row, conv_short.astype(F32),
            conv_taps, vec(conv_dw_bias), vec(conv_ln_gain), vec(conv_ln_bias),
            batch=batch, seq=seq, tm=512)
        ya = _fox_attention(aqkv, logf, batch=batch, seq=seq, tq=512)
        yd = _swa_attention(dqkv, band_bias, band_own, sinks, i, batch=batch, seq=seq, nblk=4)
        h = _merge(h, ya, yb, yc, yd, i, vec(mix_norm_pre), w_gate, vec(b_gate),
                   w_br_a, w_br_b, w_br_c, w_br_d_perm, w_o, vec(mix_norm_post), tm=FFN_TM)
        h = _ffn(h, i, vec(ffn2_norm_pre), ffn2_w_gu, ffn2_w_down, vec(ffn2_norm_post),
                 ple=(p2, vec(ple_norm_gate), w_ple_gate, w_ple, vec(ple_norm_post)), tm=FFN_TM)
    return h.reshape(batch, seq, d)
```

```python
import functools
import math

import jax
import jax.numpy as jnp
from jax import lax
from jax.experimental import pallas as pl
from jax.experimental.pallas import tpu as pltpu

F32 = jnp.float32
BF16 = jnp.bfloat16

HEAD_DIM = 64
A_HEADS = 4
A_WIDTH = A_HEADS * HEAD_DIM
B_WIDTH = 256
SHORT_CONV = 3
C_WIDTH = 256
CONF_CONV = 31
D_Q_HEADS = 8
D_KV_HEADS = 2
D_WIDTH = D_Q_HEADS * HEAD_DIM
D_KV_WIDTH = D_KV_HEADS * HEAD_DIM
WINDOW = 128
Q_BLOCK = 128
N_BRANCH = 4
REL_BUCKETS = 32
REL_MAX_DIST = 128
EPS = 1e-6
NEG_INF = -1e30
LOG2E = math.log2(math.e)

LANES = 128
SUBLANES = 8
VMEM_LIMIT = 56 * 1024 * 1024
MXU_WIDTH = 256
FFN_TM = 1024
FFN_CHUNK = MXU_WIDTH
FFN_SPLIT = 4

A_QKV_END = 3 * A_WIDTH
A_F_END = A_QKV_END + A_HEADS
B_END = A_F_END + 3 * B_WIDTH
C_END = B_END + 2 * C_WIDTH
D_END = C_END + D_WIDTH + 2 * D_KV_WIDTH

D_HEAD_PERM = (0, 4, 1, 5, 2, 6, 3, 7)

P_A = 0
P_B = P_A + 3 * A_WIDTH
P_C = P_B + 3 * B_WIDTH
P_D = P_C + 2 * C_WIDTH
P_F = P_D + D_WIDTH + 2 * D_KV_WIDTH
P_END = P_F + LANES


def _rms(x, g):
    return x * lax.rsqrt(jnp.mean(x * x, axis=-1, keepdims=True) + EPS) * g


def _dot(a, b):
    return jnp.dot(a, b, preferred_element_type=F32)


def _dot_nt(a, b):
    return lax.dot_general(a, b, (((1,), (1,)), ((), ())), preferred_element_type=F32)


def _zero_after(x):
    bits = lax.bitcast_convert_type(x, jnp.uint32)
    sixteen = jnp.uint32(16)
    bits = lax.shift_right_logical(lax.shift_right_logical(bits, sixteen), sixteen)
    return lax.bitcast_convert_type(bits, F32)


def _const_spec(shape, index):
    return pl.BlockSpec(shape, lambda *_: index, pipeline_mode=pl.Buffered(1))


def _params(n_axes):
    return pltpu.CompilerParams(dimension_semantics=("arbitrary",) * n_axes,
                                vmem_limit_bytes=VMEM_LIMIT)


def _ffn_body(*refs, d_ff, n_chunks, has_ple):
    if has_ple:
        (h_ref, p_ref, gpre_ref, wgu_ref, wdn_ref, gpost_ref,
         ggate_ref, wpg_ref, wple_ref, gple_ref, o_ref, act_ref) = refs
    else:
        h_ref, gpre_ref, wgu_ref, wdn_ref, gpost_ref, o_ref, act_ref = refs
    tm = h_ref.shape[0]
    halves = [slice(i * tm // FFN_SPLIT, (i + 1) * tm // FFN_SPLIT) for i in range(FFN_SPLIT)]
    xs = [h_ref[r, :] for r in halves]
    us = [_rms(x, gpre_ref[...]).astype(BF16) for x in xs]
    fc = d_ff // n_chunks
    for c in range(n_chunks):
        for r, u in zip(halves, us):
            gate = _dot(u, wgu_ref[:, c * fc:(c + 1) * fc])
            up = _dot(u, wgu_ref[:, d_ff + c * fc:d_ff + (c + 1) * fc])
            act_ref[r, c * fc:(c + 1) * fc] = (gate * jax.nn.sigmoid(gate) * up).astype(BF16)
    fs = [_dot(act_ref[r, :], wdn_ref[...]) for r in halves]
    ys = [x + 0.5 * _rms(f, gpost_ref[...]) for x, f in zip(xs, fs)]
    if has_ple:
        pes = [_dot(p_ref[r, :].astype(BF16), wple_ref[...]) for r in halves]
        pgs = [jax.nn.sigmoid(_dot(_rms(y, ggate_ref[...]).astype(BF16), wpg_ref[...])) for y in ys]
        ys = [y + pg * _rms(pe, gple_ref[...]) for y, pg, pe in zip(ys, pgs, pes)]
    for r, y in zip(halves, ys):
        o_ref[r, :] = y


def _ffn(h, layer, gpre, wgu, wdn, gpost, ple=None, *, tm):
    n, d = h.shape
    d_ff = wdn.shape[1]
    tm = min(tm, n)
    row = lambda t: (t, 0)
    vec = _const_spec((None, 1, d), (layer, 0, 0))
    in_specs = [pl.BlockSpec((tm, d), row)]
    args = [h]
    if ple is not None:
        p, ggate, wpg, wple, gple = ple
        in_specs.append(pl.BlockSpec((None, tm, p.shape[-1]), lambda t: (layer, t, 0)))
        args.append(p)
    in_specs += [vec, _const_spec((None, d, 2 * d_ff), (layer, 0, 0)),
                 _const_spec((None, d_ff, d), (layer, 0, 0)), vec]
    args += [gpre, wgu, wdn, gpost]
    if ple is not None:
        in_specs += [vec, _const_spec((None, d, d), (layer, 0, 0)),
                     _const_spec((None, p.shape[-1], d), (layer, 0, 0)), vec]
        args += [ggate, wpg, wple, gple]
    return pl.pallas_call(
        functools.partial(_ffn_body, d_ff=d_ff, n_chunks=d_ff // FFN_CHUNK, has_ple=ple is not None),
        grid=(n // tm,),
        in_specs=in_specs,
        out_specs=pl.BlockSpec((tm, d), row),
        out_shape=jax.ShapeDtypeStruct((n, d), F32),
        scratch_shapes=[pltpu.VMEM((tm, d_ff), BF16)],
        compiler_params=_params(1),
        name="ffn_ple" if ple is not None else "ffn",
    )(*args)


CONV_HALO = 32
SHORT_HALO = SUBLANES
CONV_ROWS = 128
CONV_SPLIT = 1
CONV_TAP_ROWS = 32


def _inproj_body(h_ref, g_ref, w_ref, bf_ref, cs_ref, cdw_ref, cdb_ref, lng_ref, lnb_ref,
                 aqkv_ref, yb_ref, yc_ref, dqkv_ref, logf_ref,
                 u_ref, zext_ref, gext_ref, gsh_ref, taps_ref):
    tm = h_ref.shape[0]

    @pl.when(pl.program_id(1) == 0)
    def _():
        zext_ref[:SHORT_HALO, :] = jnp.zeros((SHORT_HALO, B_WIDTH), F32)
        gext_ref[:CONV_HALO, :] = jnp.zeros((CONV_HALO, C_WIDTH), F32)

    u_ref[...] = _rms(h_ref[...], g_ref[...]).astype(BF16)

    c = _dot(u_ref[...], w_ref[:, P_C:P_D])
    gext_ref[CONV_HALO:, :] = c[:, :C_WIDTH] * jax.nn.sigmoid(c[:, C_WIDTH:])
    sh_rows = tm + CONV_HALO - SUBLANES
    for s in range(1, SUBLANES):
        gsh_ref[s - 1] = gext_ref[s:s + sh_rows, :]

    def conv_chunk(r0, gate):
        sub = CONV_ROWS // CONV_SPLIT
        for j in range(CONV_SPLIT):
            g_row = (j + 1) * tm // CONV_SPLIT
            zero = _zero_after(gate[g_row - SUBLANES:g_row, :LANES])
            taps_ref[...] = cdw_ref[...] + jnp.tile(zero, (CONV_TAP_ROWS // SUBLANES, C_WIDTH // LANES))
            base = r0 + j * sub
            acc = jnp.broadcast_to(cdb_ref[...], (sub, C_WIDTH))
            for k in range(CONF_CONV):
                off = CONV_HALO - (CONF_CONV - 1) + k
                s = off % SUBLANES
                lo = base + off - s
                win = gext_ref[lo:lo + sub, :] if s == 0 else gsh_ref[s - 1, lo:lo + sub, :]
                acc = acc + taps_ref[k:k + 1, :] * win
            mu = jnp.mean(acc, axis=-1, keepdims=True)
            xc = acc - mu
            y = xc * lax.rsqrt(jnp.mean(xc * xc, axis=-1, keepdims=True) + EPS)
            y = y * lng_ref[...] + lnb_ref[...]
            yc_ref[base:base + sub, :] = (y * jax.nn.sigmoid(y)).astype(BF16)

    def slab(lo):
        return _dot(u_ref[...], w_ref[:, lo:lo + MXU_WIDTH])

    def proj_a(part):
        a = slab(P_A + part * A_WIDTH)
        aqkv_ref[:, part * A_WIDTH:(part + 1) * A_WIDTH] = a.astype(BF16)
        return a

    held = {}

    def proj_b(part):
        b = slab(P_B + part * B_WIDTH)
        if part < 2:
            held[part] = b
            return b
        zext_ref[SHORT_HALO:, :] = held[1] * b
        conv = jnp.zeros((tm, B_WIDTH), F32)
        for k in range(SHORT_CONV):
            off = SHORT_HALO - (SHORT_CONV - 1) + k
            conv = conv + cs_ref[k:k + 1, :] * zext_ref[off:off + tm, :]
        yb_ref[...] = (held[0] * conv).astype(BF16)
        zext_ref[:SHORT_HALO, :] = zext_ref[tm:tm + SHORT_HALO, :]
        return b

    def proj_d(part):
        dd = slab(P_D + part * MXU_WIDTH)
        dqkv_ref[:, part * MXU_WIDTH:(part + 1) * MXU_WIDTH] = dd.astype(BF16)
        return dd

    def proj_f():
        af = _dot(u_ref[...], w_ref[:, P_F:P_END]) + bf_ref[...]
        t = af.T[:SUBLANES, :]
        logf_ref[...] = (jnp.minimum(t, 0.0) - jnp.log1p(jnp.exp(-jnp.abs(t)))) * LOG2E
        return af

    n_d = (D_WIDTH + 2 * D_KV_WIDTH) // MXU_WIDTH
    projections = ([functools.partial(proj_a, i) for i in range(3)]
                   + [functools.partial(proj_b, i) for i in range(3)]
                   + [functools.partial(proj_d, i) for i in range(n_d)] + [proj_f])
    chunks = list(range(0, tm, CONV_ROWS))
    assert len(chunks) <= len(projections)
    for i, proj in enumerate(projections):
        res = proj()
        if i < len(chunks):
            conv_chunk(chunks[i], res)
    gext_ref[:CONV_HALO, :] = gext_ref[tm:tm + CONV_HALO, :]


def _inproj(h, layer, g, w, bf, cs, cdw, cdb, lng, lnb, *, batch, seq, tm):
    n, d = h.shape
    tm = min(tm, seq)
    nt = seq // tm
    row = lambda b, t: (b * nt + t, 0)
    lay = lambda *shape: _const_spec((None,) + shape, (layer,) + (0,) * len(shape))
    out_shape = (jax.ShapeDtypeStruct((n, 3 * A_WIDTH), BF16),
                 jax.ShapeDtypeStruct((n, B_WIDTH), BF16),
                 jax.ShapeDtypeStruct((n, C_WIDTH), BF16),
                 jax.ShapeDtypeStruct((n, D_WIDTH + 2 * D_KV_WIDTH), BF16),
                 jax.ShapeDtypeStruct((batch, SUBLANES, seq), F32))
    out_specs = (pl.BlockSpec((tm, 3 * A_WIDTH), row),
                 pl.BlockSpec((tm, B_WIDTH), row),
                 pl.BlockSpec((tm, C_WIDTH), row),
                 pl.BlockSpec((tm, D_WIDTH + 2 * D_KV_WIDTH), row),
                 pl.BlockSpec((None, SUBLANES, tm), lambda b, t: (b, 0, t)))
    return pl.pallas_call(
        _inproj_body,
        grid=(batch, nt),
        in_specs=[pl.BlockSpec((tm, d), row), lay(1, d), lay(d, P_END), lay(1, LANES),
                  lay(SHORT_CONV, B_WIDTH), lay(CONV_TAP_ROWS, C_WIDTH), lay(1, C_WIDTH),
                  lay(1, C_WIDTH), lay(1, C_WIDTH)],
        out_specs=out_specs,
        out_shape=out_shape,
        scratch_shapes=[pltpu.VMEM((tm, d), BF16),
                        pltpu.VMEM((tm + SHORT_HALO, B_WIDTH), F32),
                        pltpu.VMEM((tm + CONV_HALO, C_WIDTH), F32),
                        pltpu.VMEM((SUBLANES - 1, tm + CONV_HALO - SUBLANES, C_WIDTH), F32),
                        pltpu.VMEM((CONV_TAP_ROWS, C_WIDTH), F32)],
        compiler_params=_params(2),
        name="inproj",
    )(h, g, w, bf, cs, cdw, cdb, lng, lnb)


def _split3(x):
    hi = x.astype(BF16)
    r = x - hi.astype(F32)
    mid = r.astype(BF16)
    lo = (r - mid.astype(F32)).astype(BF16)
    return hi, mid, lo


def _cumsum_lanes(x):
    rows, s = x.shape
    nb = s // LANES
    stacked = jnp.concatenate([x[:, b * LANES:(b + 1) * LANES] for b in range(nb)], axis=0)
    r = lax.broadcasted_iota(jnp.int32, (LANES, LANES), 0)
    c = lax.broadcasted_iota(jnp.int32, (LANES, LANES), 1)
    upper = jnp.where(r <= c, 1.0, 0.0).astype(BF16)
    hi, mid, lo = _split3(stacked)
    within = (_dot(lo, upper) + _dot(mid, upper)) + _dot(hi, upper)
    tot = jnp.broadcast_to(within[:, LANES - 1:], within.shape)
    n = rows * nb
    rr = lax.broadcasted_iota(jnp.int32, (n, n), 0)
    cc = lax.broadcasted_iota(jnp.int32, (n, n), 1)
    shift = rows.bit_length() - 1
    same_row = (rr & (rows - 1)) == (cc & (rows - 1))
    earlier = lax.shift_right_logical(cc, shift) < lax.shift_right_logical(rr, shift)
    prev = jnp.where(same_row & earlier, 1.0, 0.0).astype(BF16)
    hi, mid, lo = _split3(tot)
    total = within + ((_dot(prev, lo) + _dot(prev, mid)) + _dot(prev, hi))
    return [total[b * rows:(b + 1) * rows, :] for b in range(nb)]


def _fox_body(q_ref, k_ref, v_ref, logf_ref, o_ref, c_ref, *, tq, nq):
    qi = pl.program_id(1)
    per_blk = tq // LANES
    heads = range(A_HEADS)

    @pl.when(qi == 0)
    def _():
        for b, blk in enumerate(_cumsum_lanes(logf_ref[...])):
            c_ref[b // per_blk, :, (b % per_blk) * LANES:(b % per_blk + 1) * LANES] = blk

    def lane_block(ref, rows, h):
        return ref[rows, (h // 2) * LANES:(h // 2 + 1) * LANES]

    is_lo = lax.broadcasted_iota(jnp.int32, (1, LANES), 1) < HEAD_DIM
    q_heads = []
    for h in heads:
        q = lane_block(q_ref, slice(None), h)
        zero = jnp.zeros_like(q)
        q_heads.append(jnp.where(is_lo, q, zero) if h % 2 == 0 else jnp.where(is_lo, zero, q))
    c_here = c_ref[qi]
    c_end = [c_here[h:h + 1, tq - 1:tq] for h in heads]
    row_id = lax.broadcasted_iota(jnp.int32, (tq, tq), 0)
    col_id = lax.broadcasted_iota(jnp.int32, (tq, tq), 1)
    ones = jnp.ones((tq, LANES), BF16)

    def step(kj, carry, masked):
        rows = slice(kj * tq, (kj + 1) * tq)
        cb = c_ref[kj]
        scores = []
        for h in heads:
            s = _dot_nt(q_heads[h], lane_block(k_ref, rows, h)) + (c_end[h] - cb[h:h + 1, :])
            if masked:
                s = jnp.where(row_id >= col_id, s, NEG_INF)
            scores.append(s)
        weights, alphas, maxes = [], [], []
        for h in heads:
            m = carry[h][0]
            m_new = jnp.maximum(m, jnp.max(scores[h], axis=-1, keepdims=True))
            weights.append(jnp.exp2(scores[h] - m_new).astype(BF16))
            alphas.append(jnp.exp2(m - m_new))
            maxes.append(m_new)
        vvs = [jnp.concatenate([lane_block(v_ref, rows, h), ones], axis=1) for h in heads[::2]]
        return tuple((maxes[h], alphas[h] * carry[h][1] + _dot(weights[h], vvs[h // 2]))
                     for h in heads)

    init_one = (jnp.full((tq, 1), NEG_INF, F32), jnp.zeros((tq, 2 * LANES), F32))

    for n_before in range(nq):
        @pl.when(qi == n_before)
        def _(n_before=n_before):
            carry = (init_one,) * A_HEADS
            for kj in range(n_before):
                carry = step(kj, carry, False)
            acc = [a for _, a in step(n_before, carry, True)]
            out = [a[:, :LANES] * (1.0 / a[:, LANES:]) for a in acc]
            for p in range(A_HEADS // 2):
                o_ref[:, p * LANES:(p + 1) * LANES] = jnp.where(
                    is_lo, out[2 * p], out[2 * p + 1]).astype(o_ref.dtype)


def _fox_attention(aqkv, logf, *, batch, seq, tq):
    tq = min(tq, seq)
    nq = seq // tq
    qkv = aqkv.reshape(batch, seq, 3 * A_WIDTH)
    return pl.pallas_call(
        functools.partial(_fox_body, tq=tq, nq=nq),
        grid=(batch, nq),
        in_specs=[pl.BlockSpec((None, tq, A_WIDTH), lambda b, i: (b, i, 0)),
                  pl.BlockSpec((None, seq, A_WIDTH), lambda b, i: (b, 0, 1)),
                  pl.BlockSpec((None, seq, A_WIDTH), lambda b, i: (b, 0, 2)),
                  pl.BlockSpec((None, SUBLANES, seq), lambda b, i: (b, 0, 0))],
        out_specs=pl.BlockSpec((None, tq, A_WIDTH), lambda b, i: (b, i, 0)),
        out_shape=jax.ShapeDtypeStruct((batch, seq, A_WIDTH), BF16),
        scratch_shapes=[pltpu.VMEM((nq, SUBLANES, tq), F32)],
        compiler_params=_params(2),
        name="fox_attention",
    )(qkv, qkv, qkv, logf)


def _t5_causal_bucket(dist):
    max_exact = REL_BUCKETS // 2
    large = max_exact + (jnp.log(jnp.maximum(dist, 1).astype(F32) / max_exact)
                         / math.log(REL_MAX_DIST / max_exact)
                         * (REL_BUCKETS - max_exact)).astype(jnp.int32)
    large = jnp.minimum(large, REL_BUCKETS - 1)
    return jnp.where(dist < max_exact, dist, large)


def _band_bias_body(bucket_ref, rel_ref, o_ref):
    for part in range(2):
        bucket = bucket_ref[part]
        for h in range(D_Q_HEADS):
            acc = jnp.full(bucket.shape, NEG_INF, F32)
            for b in range(REL_BUCKETS):
                acc = jnp.where(bucket == b, rel_ref[b, h] * LOG2E, acc)
            o_ref[part, h] = acc


def _band_bias(rel_bias):
    dist = jnp.maximum(jnp.arange(Q_BLOCK)[:, None] + Q_BLOCK - jnp.arange(2 * Q_BLOCK)[None, :], 0)
    bucket = _t5_causal_bucket(dist).astype(jnp.int32)
    own = jnp.arange(Q_BLOCK)[None, :] <= jnp.arange(Q_BLOCK)[:, None]
    bucket = jnp.stack([jnp.where(own, bucket[:, Q_BLOCK:], -1),
                        jnp.where(own, -1, bucket[:, :Q_BLOCK])])
    bias = pl.pallas_call(
        _band_bias_body,
        in_specs=[pl.BlockSpec(memory_space=pltpu.VMEM), pl.BlockSpec(memory_space=pltpu.SMEM)],
        out_specs=pl.BlockSpec(memory_space=pltpu.VMEM),
        out_shape=jax.ShapeDtypeStruct((2, D_Q_HEADS, Q_BLOCK, Q_BLOCK), F32),
        name="band_bias",
    )(bucket, rel_bias)
    own_all = jnp.tile(own, (D_Q_HEADS, 1)).astype(BF16)
    return bias.reshape(2, D_Q_HEADS * Q_BLOCK, Q_BLOCK), own_all


def _swa_body(q_ref, kp_ref, kc_ref, vp_ref, vc_ref, bias_ref, own_ref, sink_ref, o_ref, *, nblk):
    step = pl.program_id(1)
    n_lane_blk = D_Q_HEADS // 2
    is_lo = lax.broadcasted_iota(jnp.int32, (1, LANES), 1) < HEAD_DIM
    sink = jnp.concatenate([jnp.broadcast_to(sink_ref[h:h + 1, :], (Q_BLOCK, LANES))
                            for h in range(D_Q_HEADS)], axis=0)
    ones = jnp.ones((Q_BLOCK, LANES), BF16)
    no_prev = jnp.where(step > 0, 0.0, NEG_INF)

    def block_rows(r):
        return slice(r * Q_BLOCK, (r + 1) * Q_BLOCK)

    scores = []
    for r in range(nblk):
        rows = block_rows(r)
        q_blks = [q_ref[rows, j * LANES:(j + 1) * LANES] for j in range(n_lane_blk)]
        zero = jnp.zeros_like(q_blks[0])
        qq = jnp.concatenate([jnp.where(is_lo, q, zero) for q in q_blks]
                             + [jnp.where(is_lo, zero, q) for q in q_blks], axis=0)
        kp = kp_ref[...] if r == 0 else kc_ref[block_rows(r - 1), :]
        scores.append(_dot_nt(qq, jnp.concatenate([kc_ref[rows, :], kp], axis=0)))
    weights, maxes = [], []
    for r in range(nblk):
        s2 = scores[r]
        s_own = s2[:, :Q_BLOCK] + bias_ref[0]
        s_prev = s2[:, Q_BLOCK:] + bias_ref[1]
        if r == 0:
            s_prev = s_prev + no_prev
        s = jnp.maximum(s_own, s_prev)
        m = jnp.max(jnp.maximum(s, sink), axis=-1, keepdims=True)
        e = jnp.exp2(s - m).astype(BF16)
        e_own = e * own_ref[...]
        weights.append(jnp.concatenate([e_own, e - e_own], axis=1))
        maxes.append(m)
    outs = []
    for r in range(nblk):
        rows = block_rows(r)
        vp = vp_ref[...] if r == 0 else vc_ref[block_rows(r - 1), :]
        vv = jnp.concatenate([jnp.concatenate([vc_ref[rows, :], ones], axis=1),
                              jnp.concatenate([vp, ones], axis=1)], axis=0)
        outs.append(_dot(weights[r], vv))
    half = n_lane_blk * Q_BLOCK
    for r in range(nblk):
        o2 = outs[r]
        o = o2[:, :LANES] * (1.0 / (o2[:, LANES:] + jnp.exp2(sink - maxes[r])))
        for j in range(n_lane_blk):
            lo = o[j * Q_BLOCK:(j + 1) * Q_BLOCK, :]
            hi = o[half + j * Q_BLOCK:half + (j + 1) * Q_BLOCK, :]
            o_ref[block_rows(r), j * LANES:(j + 1) * LANES] = jnp.where(is_lo, lo, hi).astype(o_ref.dtype)


def _swa_attention(dqkv, band_bias, own, sinks, layer, *, batch, seq, nblk):
    nblk = min(nblk, seq // Q_BLOCK)
    rows = nblk * Q_BLOCK
    qkv = dqkv.reshape(batch, seq, D_WIDTH + 2 * D_KV_WIDTH)
    k_blk = D_WIDTH // LANES
    v_blk = k_blk + 1
    prev = lambda t: jnp.maximum(t * nblk - 1, 0)
    return pl.pallas_call(
        functools.partial(_swa_body, nblk=nblk),
        grid=(batch, seq // rows),
        in_specs=[pl.BlockSpec((None, rows, D_WIDTH), lambda b, t: (b, t, 0)),
                  pl.BlockSpec((None, Q_BLOCK, LANES), lambda b, t: (b, prev(t), k_blk)),
                  pl.BlockSpec((None, rows, LANES), lambda b, t: (b, t, k_blk)),
                  pl.BlockSpec((None, Q_BLOCK, LANES), lambda b, t: (b, prev(t), v_blk)),
                  pl.BlockSpec((None, rows, LANES), lambda b, t: (b, t, v_blk)),
                  _const_spec((2, D_Q_HEADS * Q_BLOCK, Q_BLOCK), (0, 0, 0)),
                  _const_spec((D_Q_HEADS * Q_BLOCK, Q_BLOCK), (0, 0)),
                  _const_spec((None, D_Q_HEADS, LANES), (layer, 0, 0))],
        out_specs=pl.BlockSpec((None, rows, D_WIDTH), lambda b, t: (b, t, 0)),
        out_shape=jax.ShapeDtypeStruct((batch, seq, D_WIDTH), BF16),
        compiler_params=_params(2),
        name="swa_attention",
    )(qkv, qkv, qkv, qkv, qkv, band_bias, own, sinks)


def _merge_body(h_ref, ya_ref, yb_ref, yc_ref, yd_ref,
                gpre_ref, wg_ref, bg_ref, wa_ref, wb_ref, wc_ref, wd_ref, wo_ref, gpost_ref, o_ref):
    tm, d = h_ref.shape
    halves = [slice(i * tm // FFN_SPLIT, (i + 1) * tm // FFN_SPLIT) for i in range(FFN_SPLIT)]
    xs = [h_ref[r, :] for r in halves]
    us = [_rms(x, gpre_ref[...]).astype(BF16) for x in xs]
    branches = ((ya_ref, wa_ref), (yb_ref, wb_ref), (yc_ref, wc_ref), (yd_ref, wd_ref))
    merged = [jnp.zeros((tm // FFN_SPLIT, d), F32) for _ in halves]
    for i, (y_ref, w_ref) in enumerate(branches):
        for j, r in enumerate(halves):
            gate = jax.nn.sigmoid(_dot(us[j], wg_ref[:, i * d:(i + 1) * d]) + bg_ref[:, i * d:(i + 1) * d])
            merged[j] = merged[j] + gate * _dot(y_ref[r, :], w_ref[...])
    outs = [_dot(m.astype(BF16), wo_ref[...]) for m in merged]
    for r, x, out in zip(halves, xs, outs):
        o_ref[r, :] = x + _rms(out, gpost_ref[...])


def _merge(h, ya, yb, yc, yd, layer, gpre, wg, bg, wa, wb, wc, wd, wo, gpost, *, tm):
    n, d = h.shape
    tm = min(tm, n)
    row = lambda t: (t, 0)
    lay = lambda *shape: _const_spec((None,) + shape, (layer,) + (0,) * len(shape))
    return pl.pallas_call(
        _merge_body,
        grid=(n // tm,),
        in_specs=[pl.BlockSpec((tm, d), row),
                  pl.BlockSpec((tm, A_WIDTH), row), pl.BlockSpec((tm, B_WIDTH), row),
                  pl.BlockSpec((tm, C_WIDTH), row), pl.BlockSpec((tm, D_WIDTH), row),
                  lay(1, d), lay(d, N_BRANCH * d), lay(1, N_BRANCH * d),
                  lay(A_WIDTH, d), lay(B_WIDTH, d), lay(C_WIDTH, d), lay(D_WIDTH, d),
                  lay(d, d), lay(1, d)],
        out_specs=pl.BlockSpec((tm, d), row),
        out_shape=jax.ShapeDtypeStruct((n, d), F32),
        compiler_params=_params(1),
        name="merge",
    )(h, ya.reshape(n, A_WIDTH), yb, yc, yd.reshape(n, D_WIDTH),
      gpre, wg, bg, wa, wb, wc, wd, wo, gpost)


def kernel(x, p, ffn1_norm_pre, ffn1_w_gu, ffn1_w_down, ffn1_norm_post, mix_norm_pre, w_in, b_forget, b_gate, conv_short, conv_dw, conv_dw_bias, conv_ln_gain, conv_ln_bias, attn_sinks, rel_bias, w_br_a, w_br_b, w_br_c, w_br_d, w_o, mix_norm_post, ffn2_norm_pre, ffn2_w_gu, ffn2_w_down, ffn2_norm_post, ple_norm_gate, w_ple_gate, w_ple, ple_norm_post):
    batch, seq, d = x.shape
    depth = w_in.shape[0]
    n = batch * seq
    bf = lambda w: w.astype(BF16)
    vec = lambda g: g.astype(F32)[:, None, :]

    scale = HEAD_DIM ** -0.5 * LOG2E
    aq = bf(w_in[..., :A_WIDTH] * scale)
    dq = (w_in[..., C_END:C_END + D_WIDTH] * scale).reshape(depth, d, D_Q_HEADS, HEAD_DIM)
    dq = bf(dq[:, :, D_HEAD_PERM, :].reshape(depth, d, D_WIDTH))
    w_in = bf(w_in)
    w_pack = jnp.concatenate(
        [aq, w_in[..., A_WIDTH:A_QKV_END], w_in[..., A_F_END:C_END], dq,
         w_in[..., C_END + D_WIDTH:D_END],
         jnp.pad(w_in[..., A_QKV_END:A_F_END], ((0, 0), (0, 0), (0, LANES - A_HEADS)))], axis=-1)
    w_gate = w_in[..., D_END:]
    b_forget_row = jnp.pad(b_forget.astype(F32), ((0, 0), (0, LANES - A_HEADS)))[:, None, :]
    w_br_d_perm = bf(w_br_d.reshape(depth, D_Q_HEADS, HEAD_DIM, d)[:, D_HEAD_PERM].reshape(depth, D_WIDTH, d))
    sinks = jnp.broadcast_to(attn_sinks.astype(F32)[:, :, None] * LOG2E, (depth, D_Q_HEADS, LANES))
    band_bias, band_own = _band_bias(rel_bias.astype(F32))

    ffn1_w_gu, ffn1_w_down = bf(ffn1_w_gu), bf(ffn1_w_down)
    ffn2_w_gu, ffn2_w_down = bf(ffn2_w_gu), bf(ffn2_w_down)
    w_br_a, w_br_b, w_br_c, w_o = bf(w_br_a), bf(w_br_b), bf(w_br_c), bf(w_o)
    w_ple_gate, w_ple = bf(w_ple_gate), bf(w_ple)
    p2 = p.reshape(depth, n, p.shape[-1])
    conv_taps = jnp.pad(conv_dw.astype(F32), ((0, 0), (0, CONV_TAP_ROWS - CONF_CONV), (0, 0)))

    h = x.reshape(n, d)
    for i in range(depth):
        h = _ffn(h, i, vec(ffn1_norm_pre), ffn1_w_gu, ffn1_w_down, vec(ffn1_norm_post), tm=FFN_TM)
        aqkv, yb, yc, dqkv, logf = _inproj(
            h, i, vec(mix_norm_pre), w_pack, b_forget_row, conv_short.astype(F32),
            conv_taps, vec(conv_dw_bias), vec(conv_ln_gain), vec(conv_ln_bias),
            batch=batch, seq=seq, tm=1024)
        ya = _fox_attention(aqkv, logf, batch=batch, seq=seq, tq=512)
        yd = _swa_attention(dqkv, band_bias, band_own, sinks, i, batch=batch, seq=seq, nblk=8)
        h = _merge(h, ya, yb, yc, yd, i, vec(mix_norm_pre), w_gate, vec(b_gate),
                   w_br_a, w_br_b, w_br_c, w_br_d_perm, w_o, vec(mix_norm_post), tm=FFN_TM)
        h = _ffn(h, i, vec(ffn2_norm_pre), ffn2_w_gu, ffn2_w_down, vec(ffn2_norm_post),
                 ple=(p2, vec(ple_norm_gate), w_ple_gate, w_ple, vec(ple_norm_post)), tm=FFN_TM)
    return h.reshape(batch, seq, d)
```

```python
import functools
import math

import jax
import jax.numpy as jnp
from jax import lax
from jax.experimental import pallas as pl
from jax.experimental.pallas import tpu as pltpu

F32 = jnp.float32
BF16 = jnp.bfloat16

HEAD_DIM = 64
A_HEADS = 4
A_WIDTH = A_HEADS * HEAD_DIM
B_WIDTH = 256
SHORT_CONV = 3
C_WIDTH = 256
CONF_CONV = 31
D_Q_HEADS = 8
D_KV_HEADS = 2
D_WIDTH = D_Q_HEADS * HEAD_DIM
D_KV_WIDTH = D_KV_HEADS * HEAD_DIM
WINDOW = 128
Q_BLOCK = 128
N_BRANCH = 4
REL_BUCKETS = 32
REL_MAX_DIST = 128
EPS = 1e-6
NEG_INF = -1e30
LOG2E = math.log2(math.e)

LANES = 128
SUBLANES = 8
VMEM_LIMIT = 56 * 1024 * 1024
MXU_WIDTH = 256
FFN_TM = 1024
FFN_CHUNK = MXU_WIDTH
FFN_SPLIT = 4
INPROJ_TM = 1024
FOX_TQ = 512
SWA_BLOCKS = 16

A_QKV_END = 3 * A_WIDTH
A_F_END = A_QKV_END + A_HEADS
B_END = A_F_END + 3 * B_WIDTH
C_END = B_END + 2 * C_WIDTH
D_END = C_END + D_WIDTH + 2 * D_KV_WIDTH

D_HEAD_PERM = (0, 4, 1, 5, 2, 6, 3, 7)

P_A = 0
P_B = P_A + 3 * A_WIDTH
P_C = P_B + 3 * B_WIDTH
P_D = P_C + 2 * C_WIDTH
P_F = P_D + D_WIDTH + 2 * D_KV_WIDTH
P_END = P_F + LANES


def _rms(x, g):
    return x * lax.rsqrt(jnp.mean(x * x, axis=-1, keepdims=True) + EPS) * g


def _dot(a, b):
    return jnp.dot(a, b, preferred_element_type=F32)


def _dot_nt(a, b):
    return lax.dot_general(a, b, (((1,), (1,)), ((), ())), preferred_element_type=F32)


def _zero_after(x):
    bits = lax.bitcast_convert_type(x, jnp.uint32)
    sixteen = jnp.uint32(16)
    bits = lax.shift_right_logical(lax.shift_right_logical(bits, sixteen), sixteen)
    return lax.bitcast_convert_type(bits, F32)


def _const_spec(shape, index):
    return pl.BlockSpec(shape, lambda *_: index, pipeline_mode=pl.Buffered(1))


def _params(n_axes):
    return pltpu.CompilerParams(dimension_semantics=("arbitrary",) * n_axes,
                                vmem_limit_bytes=VMEM_LIMIT)


def _ffn_body(*refs, d_ff, n_chunks, has_ple):
    if has_ple:
        (h_ref, p_ref, gpre_ref, wgu_ref, wdn_ref, gpost_ref,
         ggate_ref, wpg_ref, wple_ref, gple_ref, o_ref, act_ref) = refs
    else:
        h_ref, gpre_ref, wgu_ref, wdn_ref, gpost_ref, o_ref, act_ref = refs
    tm = h_ref.shape[0]
    halves = [slice(i * tm // FFN_SPLIT, (i + 1) * tm // FFN_SPLIT) for i in range(FFN_SPLIT)]
    xs = [h_ref[r, :] for r in halves]
    us = [_rms(x, gpre_ref[...]).astype(BF16) for x in xs]
    fc = d_ff // n_chunks
    for c in range(n_chunks):
        for r, u in zip(halves, us):
            gate = _dot(u, wgu_ref[:, c * fc:(c + 1) * fc])
            up = _dot(u, wgu_ref[:, d_ff + c * fc:d_ff + (c + 1) * fc])
            act_ref[r, c * fc:(c + 1) * fc] = (gate * jax.nn.sigmoid(gate) * up).astype(BF16)
    fs = [_dot(act_ref[r, :], wdn_ref[...]) for r in halves]
    ys = [x + 0.5 * _rms(f, gpost_ref[...]) for x, f in zip(xs, fs)]
    if has_ple:
        pes = [_dot(p_ref[r, :].astype(BF16), wple_ref[...]) for r in halves]
        pgs = [jax.nn.sigmoid(_dot(_rms(y, ggate_ref[...]).astype(BF16), wpg_ref[...])) for y in ys]
        ys = [y + pg * _rms(pe, gple_ref[...]) for y, pg, pe in zip(ys, pgs, pes)]
    for r, y in zip(halves, ys):
        o_ref[r, :] = y


def _ffn(h, layer, gpre, wgu, wdn, gpost, ple=None, *, tm):
    n, d = h.shape
    d_ff = wdn.shape[1]
    tm = min(tm, n)
    row = lambda t: (t, 0)
    vec = _const_spec((None, 1, d), (layer, 0, 0))
    in_specs = [pl.BlockSpec((tm, d), row)]
    args = [h]
    if ple is not None:
        p, ggate, wpg, wple, gple = ple
        in_specs.append(pl.BlockSpec((None, tm, p.shape[-1]), lambda t: (layer, t, 0)))
        args.append(p)
    in_specs += [vec, _const_spec((None, d, 2 * d_ff), (layer, 0, 0)),
                 _const_spec((None, d_ff, d), (layer, 0, 0)), vec]
    args += [gpre, wgu, wdn, gpost]
    if ple is not None:
        in_specs += [vec, _const_spec((None, d, d), (layer, 0, 0)),
                     _const_spec((None, p.shape[-1], d), (layer, 0, 0)), vec]
        args += [ggate, wpg, wple, gple]
    return pl.pallas_call(
        functools.partial(_ffn_body, d_ff=d_ff, n_chunks=d_ff // FFN_CHUNK, has_ple=ple is not None),
        grid=(n // tm,),
        in_specs=in_specs,
        out_specs=pl.BlockSpec((tm, d), row),
        out_shape=jax.ShapeDtypeStruct((n, d), F32),
        scratch_shapes=[pltpu.VMEM((tm, d_ff), BF16)],
        compiler_params=_params(1),
        name="ffn_ple" if ple is not None else "ffn",
    )(*args)


CONV_HALO = 32
SHORT_HALO = SUBLANES
CONV_ROWS = 128
CONV_SPLIT = 1
CONV_TAP_ROWS = 32


def _inproj_body(h_ref, g_ref, w_ref, bf_ref, cs_ref, cdw_ref, cdb_ref, lng_ref, lnb_ref,
                 aqkv_ref, yb_ref, yc_ref, dqkv_ref, logf_ref,
                 u_ref, zext_ref, gext_ref, gsh_ref, taps_ref):
    tm = h_ref.shape[0]

    @pl.when(pl.program_id(1) == 0)
    def _():
        zext_ref[:SHORT_HALO, :] = jnp.zeros((SHORT_HALO, B_WIDTH), F32)
        gext_ref[:CONV_HALO, :] = jnp.zeros((CONV_HALO, C_WIDTH), F32)

    u_ref[...] = _rms(h_ref[...], g_ref[...]).astype(BF16)

    c = _dot(u_ref[...], w_ref[:, P_C:P_D])
    gext_ref[CONV_HALO:, :] = c[:, :C_WIDTH] * jax.nn.sigmoid(c[:, C_WIDTH:])
    sh_rows = tm + CONV_HALO - SUBLANES
    for s in range(1, SUBLANES):
        gsh_ref[s - 1] = gext_ref[s:s + sh_rows, :]

    def conv_chunk(r0, gate):
        sub = CONV_ROWS // CONV_SPLIT
        for j in range(CONV_SPLIT):
            g_row = (j + 1) * tm // CONV_SPLIT
            zero = _zero_after(gate[g_row - SUBLANES:g_row, :LANES])
            taps_ref[...] = cdw_ref[...] + jnp.tile(zero, (CONV_TAP_ROWS // SUBLANES, C_WIDTH // LANES))
            base = r0 + j * sub
            acc = jnp.broadcast_to(cdb_ref[...], (sub, C_WIDTH))
            for k in range(CONF_CONV):
                off = CONV_HALO - (CONF_CONV - 1) + k
                s = off % SUBLANES
                lo = base + off - s
                win = gext_ref[lo:lo + sub, :] if s == 0 else gsh_ref[s - 1, lo:lo + sub, :]
                acc = acc + taps_ref[k:k + 1, :] * win
            mu = jnp.mean(acc, axis=-1, keepdims=True)
            xc = acc - mu
            y = xc * lax.rsqrt(jnp.mean(xc * xc, axis=-1, keepdims=True) + EPS)
            y = y * lng_ref[...] + lnb_ref[...]
            yc_ref[base:base + sub, :] = (y * jax.nn.sigmoid(y)).astype(BF16)

    def slab(lo):
        return _dot(u_ref[...], w_ref[:, lo:lo + MXU_WIDTH])

    def proj_a(part):
        a = slab(P_A + part * A_WIDTH)
        aqkv_ref[:, part * A_WIDTH:(part + 1) * A_WIDTH] = a.astype(BF16)
        return a

    held = {}

    def proj_b(part):
        b = slab(P_B + part * B_WIDTH)
        if part < 2:
            held[part] = b
            return b
        zext_ref[SHORT_HALO:, :] = held[1] * b
        conv = jnp.zeros((tm, B_WIDTH), F32)
        for k in range(SHORT_CONV):
            off = SHORT_HALO - (SHORT_CONV - 1) + k
            conv = conv + cs_ref[k:k + 1, :] * zext_ref[off:off + tm, :]
        yb_ref[...] = (held[0] * conv).astype(BF16)
        zext_ref[:SHORT_HALO, :] = zext_ref[tm:tm + SHORT_HALO, :]
        return b

    def proj_d(part):
        dd = slab(P_D + part * MXU_WIDTH)
        dqkv_ref[:, part * MXU_WIDTH:(part + 1) * MXU_WIDTH] = dd.astype(BF16)
        return dd

    def proj_f():
        af = _dot(u_ref[...], w_ref[:, P_F:P_END]) + bf_ref[...]
        t = af.T[:SUBLANES, :]
        logf_ref[...] = (jnp.minimum(t, 0.0) - jnp.log1p(jnp.exp(-jnp.abs(t)))) * LOG2E
        return af

    n_d = (D_WIDTH + 2 * D_KV_WIDTH) // MXU_WIDTH
    projections = ([functools.partial(proj_a, i) for i in range(3)]
                   + [functools.partial(proj_b, i) for i in range(3)]
                   + [functools.partial(proj_d, i) for i in range(n_d)] + [proj_f])
    chunks = list(range(0, tm, CONV_ROWS))
    assert len(chunks) <= len(projections)
    for i, proj in enumerate(projections):
        res = proj()
        if i < len(chunks):
            conv_chunk(chunks[i], res)
    gext_ref[:CONV_HALO, :] = gext_ref[tm:tm + CONV_HALO, :]


def _inproj(h, layer, g, w, bf, cs, cdw, cdb, lng, lnb, *, batch, seq, tm):
    n, d = h.shape
    tm = min(tm, seq)
    nt = seq // tm
    row = lambda b, t: (b * nt + t, 0)
    lay = lambda *shape: _const_spec((None,) + shape, (layer,) + (0,) * len(shape))
    out_shape = (jax.ShapeDtypeStruct((n, 3 * A_WIDTH), BF16),
                 jax.ShapeDtypeStruct((n, B_WIDTH), BF16),
                 jax.ShapeDtypeStruct((n, C_WIDTH), BF16),
                 jax.ShapeDtypeStruct((n, D_WIDTH + 2 * D_KV_WIDTH), BF16),
                 jax.ShapeDtypeStruct((batch, SUBLANES, seq), F32))
    out_specs = (pl.BlockSpec((tm, 3 * A_WIDTH), row),
                 pl.BlockSpec((tm, B_WIDTH), row),
                 pl.BlockSpec((tm, C_WIDTH), row),
                 pl.BlockSpec((tm, D_WIDTH + 2 * D_KV_WIDTH), row),
                 pl.BlockSpec((None, SUBLANES, tm), lambda b, t: (b, 0, t)))
    return pl.pallas_call(
        _inproj_body,
        grid=(batch, nt),
        in_specs=[pl.BlockSpec((tm, d), row), lay(1, d), lay(d, P_END), lay(1, LANES),
                  lay(SHORT_CONV, B_WIDTH), lay(CONV_TAP_ROWS, C_WIDTH), lay(1, C_WIDTH),
                  lay(1, C_WIDTH), lay(1, C_WIDTH)],
        out_specs=out_specs,
        out_shape=out_shape,
        scratch_shapes=[pltpu.VMEM((tm, d), BF16),
                        pltpu.VMEM((tm + SHORT_HALO, B_WIDTH), F32),
                        pltpu.VMEM((tm + CONV_HALO, C_WIDTH), F32),
                        pltpu.VMEM((SUBLANES - 1, tm + CONV_HALO - SUBLANES, C_WIDTH), F32),
                        pltpu.VMEM((CONV_TAP_ROWS, C_WIDTH), F32)],
        compiler_params=_params(2),
        name="inproj",
    )(h, g, w, bf, cs, cdw, cdb, lng, lnb)


def _split3(x):
    hi = x.astype(BF16)
    r = x - hi.astype(F32)
    mid = r.astype(BF16)
    lo = (r - mid.astype(F32)).astype(BF16)
    return hi, mid, lo


def _cumsum_lanes(x):
    rows, s = x.shape
    nb = s // LANES
    stacked = jnp.concatenate([x[:, b * LANES:(b + 1) * LANES] for b in range(nb)], axis=0)
    r = lax.broadcasted_iota(jnp.int32, (LANES, LANES), 0)
    c = lax.broadcasted_iota(jnp.int32, (LANES, LANES), 1)
    upper = jnp.where(r <= c, 1.0, 0.0).astype(BF16)
    hi, mid, lo = _split3(stacked)
    within = (_dot(lo, upper) + _dot(mid, upper)) + _dot(hi, upper)
    tot = jnp.broadcast_to(within[:, LANES - 1:], within.shape)
    n = rows * nb
    rr = lax.broadcasted_iota(jnp.int32, (n, n), 0)
    cc = lax.broadcasted_iota(jnp.int32, (n, n), 1)
    shift = rows.bit_length() - 1
    same_row = (rr & (rows - 1)) == (cc & (rows - 1))
    earlier = lax.shift_right_logical(cc, shift) < lax.shift_right_logical(rr, shift)
    prev = jnp.where(same_row & earlier, 1.0, 0.0).astype(BF16)
    hi, mid, lo = _split3(tot)
    total = within + ((_dot(prev, lo) + _dot(prev, mid)) + _dot(prev, hi))
    return [total[b * rows:(b + 1) * rows, :] for b in range(nb)]


def _fox_body(q_ref, k_ref, v_ref, logf_ref, o_ref, c_ref, *, tq, nq):
    qi = pl.program_id(1)
    per_blk = tq // LANES
    heads = range(A_HEADS)

    @pl.when(qi == 0)
    def _():
        for b, blk in enumerate(_cumsum_lanes(logf_ref[...])):
            c_ref[b // per_blk, :, (b % per_blk) * LANES:(b % per_blk + 1) * LANES] = blk

    def lane_block(ref, rows, h):
        return ref[rows, (h // 2) * LANES:(h // 2 + 1) * LANES]

    is_lo = lax.broadcasted_iota(jnp.int32, (1, LANES), 1) < HEAD_DIM
    q_heads = []
    for h in heads:
        q = lane_block(q_ref, slice(None), h)
        zero = jnp.zeros_like(q)
        q_heads.append(jnp.where(is_lo, q, zero) if h % 2 == 0 else jnp.where(is_lo, zero, q))
    c_here = c_ref[qi]
    c_end = [c_here[h:h + 1, tq - 1:tq] for h in heads]
    row_id = lax.broadcasted_iota(jnp.int32, (tq, tq), 0)
    col_id = lax.broadcasted_iota(jnp.int32, (tq, tq), 1)
    ones = jnp.ones((tq, LANES), BF16)

    def step(kj, carry, masked):
        rows = slice(kj * tq, (kj + 1) * tq)
        cb = c_ref[kj]
        scores = []
        for h in heads:
            s = _dot_nt(q_heads[h], lane_block(k_ref, rows, h)) + (c_end[h] - cb[h:h + 1, :])
            if masked:
                s = jnp.where(row_id >= col_id, s, NEG_INF)
            scores.append(s)
        weights, alphas, maxes = [], [], []
        for h in heads:
            m = carry[h][0]
            m_new = jnp.maximum(m, jnp.max(scores[h], axis=-1, keepdims=True))
            weights.append(jnp.exp2(scores[h] - m_new).astype(BF16))
            alphas.append(jnp.exp2(m - m_new))
            maxes.append(m_new)
        vvs = [jnp.concatenate([lane_block(v_ref, rows, h), ones], axis=1) for h in heads[::2]]
        return tuple((maxes[h], alphas[h] * carry[h][1] + _dot(weights[h], vvs[h // 2]))
                     for h in heads)

    init_one = (jnp.full((tq, 1), NEG_INF, F32), jnp.zeros((tq, 2 * LANES), F32))

    for n_before in range(nq):
        @pl.when(qi == n_before)
        def _(n_before=n_before):
            carry = (init_one,) * A_HEADS
            for kj in range(n_before):
                carry = step(kj, carry, False)
            acc = [a for _, a in step(n_before, carry, True)]
            out = [a[:, :LANES] * (1.0 / a[:, LANES:]) for a in acc]
            for p in range(A_HEADS // 2):
                o_ref[:, p * LANES:(p + 1) * LANES] = jnp.where(
                    is_lo, out[2 * p], out[2 * p + 1]).astype(o_ref.dtype)


def _fox_attention(aqkv, logf, *, batch, seq, tq):
    tq = min(tq, seq)
    nq = seq // tq
    qkv = aqkv.reshape(batch, seq, 3 * A_WIDTH)
    return pl.pallas_call(
        functools.partial(_fox_body, tq=tq, nq=nq),
        grid=(batch, nq),
        in_specs=[pl.BlockSpec((None, tq, A_WIDTH), lambda b, i: (b, i, 0)),
                  pl.BlockSpec((None, seq, A_WIDTH), lambda b, i: (b, 0, 1)),
                  pl.BlockSpec((None, seq, A_WIDTH), lambda b, i: (b, 0, 2)),
                  pl.BlockSpec((None, SUBLANES, seq), lambda b, i: (b, 0, 0))],
        out_specs=pl.BlockSpec((None, tq, A_WIDTH), lambda b, i: (b, i, 0)),
        out_shape=jax.ShapeDtypeStruct((batch, seq, A_WIDTH), BF16),
        scratch_shapes=[pltpu.VMEM((nq, SUBLANES, tq), F32)],
        compiler_params=_params(2),
        name="fox_attention",
    )(qkv, qkv, qkv, logf)


def _t5_causal_bucket(dist):
    max_exact = REL_BUCKETS // 2
    large = max_exact + (jnp.log(jnp.maximum(dist, 1).astype(F32) / max_exact)
                         / math.log(REL_MAX_DIST / max_exact)
                         * (REL_BUCKETS - max_exact)).astype(jnp.int32)
    large = jnp.minimum(large, REL_BUCKETS - 1)
    return jnp.where(dist < max_exact, dist, large)


def _band_bias_body(bucket_ref, rel_ref, o_ref):
    for part in range(2):
        bucket = bucket_ref[part]
        for h in range(D_Q_HEADS):
            acc = jnp.full(bucket.shape, NEG_INF, F32)
            for b in range(REL_BUCKETS):
                acc = jnp.where(bucket == b, rel_ref[b, h] * LOG2E, acc)
            o_ref[part, h] = acc


def _band_bias(rel_bias):
    dist = jnp.maximum(jnp.arange(Q_BLOCK)[:, None] + Q_BLOCK - jnp.arange(2 * Q_BLOCK)[None, :], 0)
    bucket = _t5_causal_bucket(dist).astype(jnp.int32)
    own = jnp.arange(Q_BLOCK)[None, :] <= jnp.arange(Q_BLOCK)[:, None]
    bucket = jnp.stack([jnp.where(own, bucket[:, Q_BLOCK:], -1),
                        jnp.where(own, -1, bucket[:, :Q_BLOCK])])
    bias = pl.pallas_call(
        _band_bias_body,
        in_specs=[pl.BlockSpec(memory_space=pltpu.VMEM), pl.BlockSpec(memory_space=pltpu.SMEM)],
        out_specs=pl.BlockSpec(memory_space=pltpu.VMEM),
        out_shape=jax.ShapeDtypeStruct((2, D_Q_HEADS, Q_BLOCK, Q_BLOCK), F32),
        name="band_bias",
    )(bucket, rel_bias)
    own_all = jnp.tile(own, (D_Q_HEADS, 1)).astype(BF16)
    return bias.reshape(2, D_Q_HEADS * Q_BLOCK, Q_BLOCK), own_all


def _swa_body(q_ref, kp_ref, kc_ref, vp_ref, vc_ref, bias_ref, own_ref, sink_ref, o_ref, *, nblk):
    step = pl.program_id(1)
    n_lane_blk = D_Q_HEADS // 2
    is_lo = lax.broadcasted_iota(jnp.int32, (1, LANES), 1) < HEAD_DIM
    sink = jnp.concatenate([jnp.broadcast_to(sink_ref[h:h + 1, :], (Q_BLOCK, LANES))
                            for h in range(D_Q_HEADS)], axis=0)
    ones = jnp.ones((Q_BLOCK, LANES), BF16)
    no_prev = jnp.where(step > 0, 0.0, NEG_INF)

    def block_rows(r):
        return slice(r * Q_BLOCK, (r + 1) * Q_BLOCK)

    scores = []
    for r in range(nblk):
        rows = block_rows(r)
        q_blks = [q_ref[rows, j * LANES:(j + 1) * LANES] for j in range(n_lane_blk)]
        zero = jnp.zeros_like(q_blks[0])
        qq = jnp.concatenate([jnp.where(is_lo, q, zero) for q in q_blks]
                             + [jnp.where(is_lo, zero, q) for q in q_blks], axis=0)
        kp = kp_ref[...] if r == 0 else kc_ref[block_rows(r - 1), :]
        scores.append(_dot_nt(qq, jnp.concatenate([kc_ref[rows, :], kp], axis=0)))
    weights, maxes = [], []
    for r in range(nblk):
        s2 = scores[r]
        s_own = s2[:, :Q_BLOCK] + bias_ref[0]
        s_prev = s2[:, Q_BLOCK:] + bias_ref[1]
        if r == 0:
            s_prev = s_prev + no_prev
        s = jnp.maximum(s_own, s_prev)
        m = jnp.max(jnp.maximum(s, sink), axis=-1, keepdims=True)
        e = jnp.exp2(s - m).astype(BF16)
        e_own = e * own_ref[...]
        weights.append(jnp.concatenate([e_own, e - e_own], axis=1))
        maxes.append(m)
    outs = []
    for r in range(nblk):
        rows = block_rows(r)
        vp = vp_ref[...] if r == 0 else vc_ref[block_rows(r - 1), :]
        vv = jnp.concatenate([jnp.concatenate([vc_ref[rows, :], ones], axis=1),
                              jnp.concatenate([vp, ones], axis=1)], axis=0)
        outs.append(_dot(weights[r], vv))
    half = n_lane_blk * Q_BLOCK
    for r in range(nblk):
        o2 = outs[r]
        o = o2[:, :LANES] * (1.0 / (o2[:, LANES:] + jnp.exp2(sink - maxes[r])))
        for j in range(n_lane_blk):
            lo = o[j * Q_BLOCK:(j + 1) * Q_BLOCK, :]
            hi = o[half + j * Q_BLOCK:half + (j + 1) * Q_BLOCK, :]
            o_ref[block_rows(r), j * LANES:(j + 1) * LANES] = jnp.where(is_lo, lo, hi).astype(o_ref.dtype)


def _swa_attention(dqkv, band_bias, own, sinks, layer, *, batch, seq, nblk):
    nblk = min(nblk, seq // Q_BLOCK)
    rows = nblk * Q_BLOCK
    qkv = dqkv.reshape(batch, seq, D_WIDTH + 2 * D_KV_WIDTH)
    k_blk = D_WIDTH // LANES
    v_blk = k_blk + 1
    prev = lambda t: jnp.maximum(t * nblk - 1, 0)
    return pl.pallas_call(
        functools.partial(_swa_body, nblk=nblk),
        grid=(batch, seq // rows),
        in_specs=[pl.BlockSpec((None, rows, D_WIDTH), lambda b, t: (b, t, 0)),
                  pl.BlockSpec((None, Q_BLOCK, LANES), lambda b, t: (b, prev(t), k_blk)),
                  pl.BlockSpec((None, rows, LANES), lambda b, t: (b, t, k_blk)),
                  pl.BlockSpec((None, Q_BLOCK, LANES), lambda b, t: (b, prev(t), v_blk)),
                  pl.BlockSpec((None, rows, LANES), lambda b, t: (b, t, v_blk)),
                  _const_spec((2, D_Q_HEADS * Q_BLOCK, Q_BLOCK), (0, 0, 0)),
                  _const_spec((D_Q_HEADS * Q_BLOCK, Q_BLOCK), (0, 0)),
                  _const_spec((None, D_Q_HEADS, LANES), (layer, 0, 0))],
        out_specs=pl.BlockSpec((None, rows, D_WIDTH), lambda b, t: (b, t, 0)),
        out_shape=jax.ShapeDtypeStruct((batch, seq, D_WIDTH), BF16),
        compiler_params=_params(2),
        name="swa_attention",
    )(qkv, qkv, qkv, qkv, qkv, band_bias, own, sinks)


def _merge_body(h_ref, ya_ref, yb_ref, yc_ref, yd_ref,
                gpre_ref, wg_ref, bg_ref, wa_ref, wb_ref, wc_ref, wd_ref, wo_ref, gpost_ref, o_ref):
    tm, d = h_ref.shape
    halves = [slice(i * tm // FFN_SPLIT, (i + 1) * tm // FFN_SPLIT) for i in range(FFN_SPLIT)]
    xs = [h_ref[r, :] for r in halves]
    us = [_rms(x, gpre_ref[...]).astype(BF16) for x in xs]
    branches = ((ya_ref, wa_ref), (yb_ref, wb_ref), (yc_ref, wc_ref), (yd_ref, wd_ref))
    merged = [jnp.zeros((tm // FFN_SPLIT, d), F32) for _ in halves]
    for i, (y_ref, w_ref) in enumerate(branches):
        for j, r in enumerate(halves):
            gate = jax.nn.sigmoid(_dot(us[j], wg_ref[:, i * d:(i + 1) * d]) + bg_ref[:, i * d:(i + 1) * d])
            merged[j] = merged[j] + gate * _dot(y_ref[r, :], w_ref[...])
    outs = [_dot(m.astype(BF16), wo_ref[...]) for m in merged]
    for r, x, out in zip(halves, xs, outs):
        o_ref[r, :] = x + _rms(out, gpost_ref[...])


def _merge(h, ya, yb, yc, yd, layer, gpre, wg, bg, wa, wb, wc, wd, wo, gpost, *, tm):
    n, d = h.shape
    tm = min(tm, n)
    row = lambda t: (t, 0)
    lay = lambda *shape: _const_spec((None,) + shape, (layer,) + (0,) * len(shape))
    return pl.pallas_call(
        _merge_body,
        grid=(n // tm,),
        in_specs=[pl.BlockSpec((tm, d), row),
                  pl.BlockSpec((tm, A_WIDTH), row), pl.BlockSpec((tm, B_WIDTH), row),
                  pl.BlockSpec((tm, C_WIDTH), row), pl.BlockSpec((tm, D_WIDTH), row),
                  lay(1, d), lay(d, N_BRANCH * d), lay(1, N_BRANCH * d),
                  lay(A_WIDTH, d), lay(B_WIDTH, d), lay(C_WIDTH, d), lay(D_WIDTH, d),
                  lay(d, d), lay(1, d)],
        out_specs=pl.BlockSpec((tm, d), row),
        out_shape=jax.ShapeDtypeStruct((n, d), F32),
        compiler_params=_params(1),
        name="merge",
    )(h, ya.reshape(n, A_WIDTH), yb, yc, yd.reshape(n, D_WIDTH),
      gpre, wg, bg, wa, wb, wc, wd, wo, gpost)


def kernel(x, p, ffn1_norm_pre, ffn1_w_gu, ffn1_w_down, ffn1_norm_post, mix_norm_pre, w_in, b_forget, b_gate, conv_short, conv_dw, conv_dw_bias, conv_ln_gain, conv_ln_bias, attn_sinks, rel_bias, w_br_a, w_br_b, w_br_c, w_br_d, w_o, mix_norm_post, ffn2_norm_pre, ffn2_w_gu, ffn2_w_down, ffn2_norm_post, ple_norm_gate, w_ple_gate, w_ple, ple_norm_post):
    batch, seq, d = x.shape
    depth = w_in.shape[0]
    n = batch * seq
    bf = lambda w: w.astype(BF16)
    vec = lambda g: g.astype(F32)[:, None, :]

    scale = HEAD_DIM ** -0.5 * LOG2E
    aq = bf(w_in[..., :A_WIDTH] * scale)
    dq = (w_in[..., C_END:C_END + D_WIDTH] * scale).reshape(depth, d, D_Q_HEADS, HEAD_DIM)
    dq = bf(dq[:, :, D_HEAD_PERM, :].reshape(depth, d, D_WIDTH))
    w_in = bf(w_in)
    w_pack = jnp.concatenate(
        [aq, w_in[..., A_WIDTH:A_QKV_END], w_in[..., A_F_END:C_END], dq,
         w_in[..., C_END + D_WIDTH:D_END],
         jnp.pad(w_in[..., A_QKV_END:A_F_END], ((0, 0), (0, 0), (0, LANES - A_HEADS)))], axis=-1)
    w_gate = w_in[..., D_END:]
    b_forget_row = jnp.pad(b_forget.astype(F32), ((0, 0), (0, LANES - A_HEADS)))[:, None, :]
    w_br_d_perm = bf(w_br_d.reshape(depth, D_Q_HEADS, HEAD_DIM, d)[:, D_HEAD_PERM].reshape(depth, D_WIDTH, d))
    sinks = jnp.broadcast_to(attn_sinks.astype(F32)[:, :, None] * LOG2E, (depth, D_Q_HEADS, LANES))
    band_bias, band_own = _band_bias(rel_bias.astype(F32))

    ffn1_w_gu, ffn1_w_down = bf(ffn1_w_gu), bf(ffn1_w_down)
    ffn2_w_gu, ffn2_w_down = bf(ffn2_w_gu), bf(ffn2_w_down)
    w_br_a, w_br_b, w_br_c, w_o = bf(w_br_a), bf(w_br_b), bf(w_br_c), bf(w_o)
    w_ple_gate, w_ple = bf(w_ple_gate), bf(w_ple)
    p2 = p.reshape(depth, n, p.shape[-1])
    conv_taps = jnp.pad(conv_dw.astype(F32), ((0, 0), (0, CONV_TAP_ROWS - CONF_CONV), (0, 0)))

    h = x.reshape(n, d)
    for i in range(depth):
        h = _ffn(h, i, vec(ffn1_norm_pre), ffn1_w_gu, ffn1_w_down, vec(ffn1_norm_post), tm=FFN_TM)
        aqkv, yb, yc, dqkv, logf = _inproj(
            h, i, vec(mix_norm_pre), w_pack, b_forget_row, conv_short.astype(F32),
            conv_taps, vec(conv_dw_bias), vec(conv_ln_gain), vec(conv_ln_bias),
            batch=batch, seq=seq, tm=INPROJ_TM)
        ya = _fox_attention(aqkv, logf, batch=batch, seq=seq, tq=FOX_TQ)
        yd = _swa_attention(dqkv, band_bias, band_own, sinks, i, batch=batch, seq=seq,
                            nblk=SWA_BLOCKS)
        h = _merge(h, ya, yb, yc, yd, i, vec(mix_norm_pre), w_gate, vec(b_gate),
                   w_br_a, w_br_b, w_br_c, w_br_d_perm, w_o, vec(mix_norm_post), tm=FFN_TM)
        h = _ffn(h, i, vec(ffn2_norm_pre), ffn2_w_gu, ffn2_w_down, vec(ffn2_norm_post),
                 ple=(p2, vec(ple_norm_gate), w_ple_gate, w_ple, vec(ple_norm_post)), tm=FFN_TM)
    return h.reshape(batch, seq, d)
```

```python
import functools
import math

import jax
import jax.numpy as jnp
from jax import lax
from jax.experimental import pallas as pl
from jax.experimental.pallas import tpu as pltpu

F32 = jnp.float32
BF16 = jnp.bfloat16

HEAD_DIM = 64
A_HEADS = 4
A_WIDTH = A_HEADS * HEAD_DIM
B_WIDTH = 256
SHORT_CONV = 3
C_WIDTH = 256
CONF_CONV = 31
D_Q_HEADS = 8
D_KV_HEADS = 2
D_WIDTH = D_Q_HEADS * HEAD_DIM
D_KV_WIDTH = D_KV_HEADS * HEAD_DIM
WINDOW = 128
Q_BLOCK = 128
N_BRANCH = 4
REL_BUCKETS = 32
REL_MAX_DIST = 128
EPS = 1e-6
NEG_INF = -1e30
LOG2E = math.log2(math.e)

LANES = 128
SUBLANES = 8
VMEM_LIMIT = 56 * 1024 * 1024
MXU_WIDTH = 256
FFN_TM = 1024
FFN_CHUNK = MXU_WIDTH
FFN_SPLIT = 4
INPROJ_TM = 1024
FOX_TQ = 512
SWA_GROUP = 8

A_QKV_END = 3 * A_WIDTH
A_F_END = A_QKV_END + A_HEADS
B_END = A_F_END + 3 * B_WIDTH
C_END = B_END + 2 * C_WIDTH
D_END = C_END + D_WIDTH + 2 * D_KV_WIDTH

D_HEAD_PERM = (0, 4, 1, 5, 2, 6, 3, 7)

P_A = 0
P_B = P_A + 3 * A_WIDTH
P_C = P_B + 3 * B_WIDTH
P_D = P_C + 2 * C_WIDTH
P_F = P_D + D_WIDTH + 2 * D_KV_WIDTH
P_END = P_F + LANES


def _rms(x, g):
    return x * lax.rsqrt(jnp.mean(x * x, axis=-1, keepdims=True) + EPS) * g


def _dot(a, b):
    return jnp.dot(a, b, preferred_element_type=F32)


def _dot_nt(a, b):
    return lax.dot_general(a, b, (((1,), (1,)), ((), ())), preferred_element_type=F32)


def _zero_after(x):
    bits = lax.bitcast_convert_type(x, jnp.uint32)
    sixteen = jnp.uint32(16)
    bits = lax.shift_right_logical(lax.shift_right_logical(bits, sixteen), sixteen)
    return lax.bitcast_convert_type(bits, F32)


def _const_spec(shape, index):
    return pl.BlockSpec(shape, lambda *_: index, pipeline_mode=pl.Buffered(1))


def _params(n_axes):
    return pltpu.CompilerParams(dimension_semantics=("arbitrary",) * n_axes,
                                vmem_limit_bytes=VMEM_LIMIT)


def _ffn_body(*refs, d_ff, n_chunks, has_ple):
    if has_ple:
        (h_ref, p_ref, gpre_ref, wgu_ref, wdn_ref, gpost_ref,
         ggate_ref, wpg_ref, wple_ref, gple_ref, o_ref, act_ref) = refs
    else:
        h_ref, gpre_ref, wgu_ref, wdn_ref, gpost_ref, o_ref, act_ref = refs
    tm = h_ref.shape[0]
    halves = [slice(i * tm // FFN_SPLIT, (i + 1) * tm // FFN_SPLIT) for i in range(FFN_SPLIT)]
    xs = [h_ref[r, :] for r in halves]
    us = [_rms(x, gpre_ref[...]).astype(BF16) for x in xs]
    fc = d_ff // n_chunks
    for c in range(n_chunks):
        for r, u in zip(halves, us):
            gate = _dot(u, wgu_ref[:, c * fc:(c + 1) * fc])
            up = _dot(u, wgu_ref[:, d_ff + c * fc:d_ff + (c + 1) * fc])
            act_ref[r, c * fc:(c + 1) * fc] = (gate * jax.nn.sigmoid(gate) * up).astype(BF16)
    fs = [_dot(act_ref[r, :], wdn_ref[...]) for r in halves]
    ys = [x + 0.5 * _rms(f, gpost_ref[...]) for x, f in zip(xs, fs)]
    if has_ple:
        pes = [_dot(p_ref[r, :].astype(BF16), wple_ref[...]) for r in halves]
        pgs = [jax.nn.sigmoid(_dot(_rms(y, ggate_ref[...]).astype(BF16), wpg_ref[...])) for y in ys]
        ys = [y + pg * _rms(pe, gple_ref[...]) for y, pg, pe in zip(ys, pgs, pes)]
    for r, y in zip(halves, ys):
        o_ref[r, :] = y


def _ffn(h, layer, gpre, wgu, wdn, gpost, ple=None, *, tm):
    n, d = h.shape
    d_ff = wdn.shape[1]
    tm = min(tm, n)
    row = lambda t: (t, 0)
    vec = _const_spec((None, 1, d), (layer, 0, 0))
    in_specs = [pl.BlockSpec((tm, d), row)]
    args = [h]
    if ple is not None:
        p, ggate, wpg, wple, gple = ple
        in_specs.append(pl.BlockSpec((None, tm, p.shape[-1]), lambda t: (layer, t, 0)))
        args.append(p)
    in_specs += [vec, _const_spec((None, d, 2 * d_ff), (layer, 0, 0)),
                 _const_spec((None, d_ff, d), (layer, 0, 0)), vec]
    args += [gpre, wgu, wdn, gpost]
    if ple is not None:
        in_specs += [vec, _const_spec((None, d, d), (layer, 0, 0)),
                     _const_spec((None, p.shape[-1], d), (layer, 0, 0)), vec]
        args += [ggate, wpg, wple, gple]
    return pl.pallas_call(
        functools.partial(_ffn_body, d_ff=d_ff, n_chunks=d_ff // FFN_CHUNK, has_ple=ple is not None),
        grid=(n // tm,),
        in_specs=in_specs,
        out_specs=pl.BlockSpec((tm, d), row),
        out_shape=jax.ShapeDtypeStruct((n, d), F32),
        scratch_shapes=[pltpu.VMEM((tm, d_ff), BF16)],
        compiler_params=_params(1),
        name="ffn_ple" if ple is not None else "ffn",
    )(*args)


CONV_HALO = 32
SHORT_HALO = SUBLANES
CONV_ROWS = 128
CONV_SPLIT = 1
CONV_TAP_ROWS = 32


def _inproj_body(h_ref, g_ref, w_ref, bf_ref, cs_ref, cdw_ref, cdb_ref, lng_ref, lnb_ref,
                 aqkv_ref, yb_ref, yc_ref, dqkv_ref, logf_ref,
                 u_ref, zext_ref, gext_ref, gsh_ref, taps_ref):
    tm = h_ref.shape[0]

    @pl.when(pl.program_id(1) == 0)
    def _():
        zext_ref[:SHORT_HALO, :] = jnp.zeros((SHORT_HALO, B_WIDTH), F32)
        gext_ref[:CONV_HALO, :] = jnp.zeros((CONV_HALO, C_WIDTH), F32)

    u_ref[...] = _rms(h_ref[...], g_ref[...]).astype(BF16)

    c = _dot(u_ref[...], w_ref[:, P_C:P_D])
    gext_ref[CONV_HALO:, :] = c[:, :C_WIDTH] * jax.nn.sigmoid(c[:, C_WIDTH:])
    sh_rows = tm + CONV_HALO - SUBLANES
    for s in range(1, SUBLANES):
        gsh_ref[s - 1] = gext_ref[s:s + sh_rows, :]

    def conv_chunk(r0, gate):
        sub = CONV_ROWS // CONV_SPLIT
        for j in range(CONV_SPLIT):
            g_row = (j + 1) * tm // CONV_SPLIT
            zero = _zero_after(gate[g_row - SUBLANES:g_row, :LANES])
            taps_ref[...] = cdw_ref[...] + jnp.tile(zero, (CONV_TAP_ROWS // SUBLANES, C_WIDTH // LANES))
            base = r0 + j * sub
            acc = jnp.broadcast_to(cdb_ref[...], (sub, C_WIDTH))
            for k in range(CONF_CONV):
                off = CONV_HALO - (CONF_CONV - 1) + k
                s = off % SUBLANES
                lo = base + off - s
                win = gext_ref[lo:lo + sub, :] if s == 0 else gsh_ref[s - 1, lo:lo + sub, :]
                acc = acc + taps_ref[k:k + 1, :] * win
            mu = jnp.mean(acc, axis=-1, keepdims=True)
            xc = acc - mu
            y = xc * lax.rsqrt(jnp.mean(xc * xc, axis=-1, keepdims=True) + EPS)
            y = y * lng_ref[...] + lnb_ref[...]
            yc_ref[base:base + sub, :] = (y * jax.nn.sigmoid(y)).astype(BF16)

    def slab(lo):
        return _dot(u_ref[...], w_ref[:, lo:lo + MXU_WIDTH])

    def proj_a(part):
        a = slab(P_A + part * A_WIDTH)
        aqkv_ref[:, part * A_WIDTH:(part + 1) * A_WIDTH] = a.astype(BF16)
        return a

    held = {}

    def proj_b(part):
        b = slab(P_B + part * B_WIDTH)
        if part < 2:
            held[part] = b
            return b
        zext_ref[SHORT_HALO:, :] = held[1] * b
        conv = jnp.zeros((tm, B_WIDTH), F32)
        for k in range(SHORT_CONV):
            off = SHORT_HALO - (SHORT_CONV - 1) + k
            conv = conv + cs_ref[k:k + 1, :] * zext_ref[off:off + tm, :]
        yb_ref[...] = (held[0] * conv).astype(BF16)
        zext_ref[:SHORT_HALO, :] = zext_ref[tm:tm + SHORT_HALO, :]
        return b

    def proj_d(part):
        dd = slab(P_D + part * MXU_WIDTH)
        dqkv_ref[:, part * MXU_WIDTH:(part + 1) * MXU_WIDTH] = dd.astype(BF16)
        return dd

    def proj_f():
        af = _dot(u_ref[...], w_ref[:, P_F:P_END]) + bf_ref[...]
        t = af.T[:SUBLANES, :]
        logf_ref[...] = (jnp.minimum(t, 0.0) - jnp.log1p(jnp.exp(-jnp.abs(t)))) * LOG2E
        return af

    n_d = (D_WIDTH + 2 * D_KV_WIDTH) // MXU_WIDTH
    projections = ([functools.partial(proj_a, i) for i in range(3)]
                   + [functools.partial(proj_b, i) for i in range(3)]
                   + [functools.partial(proj_d, i) for i in range(n_d)] + [proj_f])
    chunks = list(range(0, tm, CONV_ROWS))
    assert len(chunks) <= len(projections)
    for i, proj in enumerate(projections):
        res = proj()
        if i < len(chunks):
            conv_chunk(chunks[i], res)
    gext_ref[:CONV_HALO, :] = gext_ref[tm:tm + CONV_HALO, :]


def _inproj(h, layer, g, w, bf, cs, cdw, cdb, lng, lnb, *, batch, seq, tm):
    n, d = h.shape
    tm = min(tm, seq)
    nt = seq // tm
    row = lambda b, t: (b * nt + t, 0)
    lay = lambda *shape: _const_spec((None,) + shape, (layer,) + (0,) * len(shape))
    out_shape = (jax.ShapeDtypeStruct((n, 3 * A_WIDTH), BF16),
                 jax.ShapeDtypeStruct((n, B_WIDTH), BF16),
                 jax.ShapeDtypeStruct((n, C_WIDTH), BF16),
                 jax.ShapeDtypeStruct((n, D_WIDTH + 2 * D_KV_WIDTH), BF16),
                 jax.ShapeDtypeStruct((batch, SUBLANES, seq), F32))
    out_specs = (pl.BlockSpec((tm, 3 * A_WIDTH), row),
                 pl.BlockSpec((tm, B_WIDTH), row),
                 pl.BlockSpec((tm, C_WIDTH), row),
                 pl.BlockSpec((tm, D_WIDTH + 2 * D_KV_WIDTH), row),
                 pl.BlockSpec((None, SUBLANES, tm), lambda b, t: (b, 0, t)))
    return pl.pallas_call(
        _inproj_body,
        grid=(batch, nt),
        in_specs=[pl.BlockSpec((tm, d), row), lay(1, d), lay(d, P_END), lay(1, LANES),
                  lay(SHORT_CONV, B_WIDTH), lay(CONV_TAP_ROWS, C_WIDTH), lay(1, C_WIDTH),
                  lay(1, C_WIDTH), lay(1, C_WIDTH)],
        out_specs=out_specs,
        out_shape=out_shape,
        scratch_shapes=[pltpu.VMEM((tm, d), BF16),
                        pltpu.VMEM((tm + SHORT_HALO, B_WIDTH), F32),
                        pltpu.VMEM((tm + CONV_HALO, C_WIDTH), F32),
                        pltpu.VMEM((SUBLANES - 1, tm + CONV_HALO - SUBLANES, C_WIDTH), F32),
                        pltpu.VMEM((CONV_TAP_ROWS, C_WIDTH), F32)],
        compiler_params=_params(2),
        name="inproj",
    )(h, g, w, bf, cs, cdw, cdb, lng, lnb)


def _split3(x):
    hi = x.astype(BF16)
    r = x - hi.astype(F32)
    mid = r.astype(BF16)
    lo = (r - mid.astype(F32)).astype(BF16)
    return hi, mid, lo


def _cumsum_lanes(x):
    rows, s = x.shape
    nb = s // LANES
    stacked = jnp.concatenate([x[:, b * LANES:(b + 1) * LANES] for b in range(nb)], axis=0)
    r = lax.broadcasted_iota(jnp.int32, (LANES, LANES), 0)
    c = lax.broadcasted_iota(jnp.int32, (LANES, LANES), 1)
    upper = jnp.where(r <= c, 1.0, 0.0).astype(BF16)
    hi, mid, lo = _split3(stacked)
    within = (_dot(lo, upper) + _dot(mid, upper)) + _dot(hi, upper)
    tot = jnp.broadcast_to(within[:, LANES - 1:], within.shape)
    n = rows * nb
    rr = lax.broadcasted_iota(jnp.int32, (n, n), 0)
    cc = lax.broadcasted_iota(jnp.int32, (n, n), 1)
    shift = rows.bit_length() - 1
    same_row = (rr & (rows - 1)) == (cc & (rows - 1))
    earlier = lax.shift_right_logical(cc, shift) < lax.shift_right_logical(rr, shift)
    prev = jnp.where(same_row & earlier, 1.0, 0.0).astype(BF16)
    hi, mid, lo = _split3(tot)
    total = within + ((_dot(prev, lo) + _dot(prev, mid)) + _dot(prev, hi))
    return [total[b * rows:(b + 1) * rows, :] for b in range(nb)]


def _fox_body(q_ref, k_ref, v_ref, logf_ref, o_ref, c_ref, *, tq, nq):
    qi = pl.program_id(1)
    per_blk = tq // LANES
    heads = range(A_HEADS)

    @pl.when(qi == 0)
    def _():
        for b, blk in enumerate(_cumsum_lanes(logf_ref[...])):
            c_ref[b // per_blk, :, (b % per_blk) * LANES:(b % per_blk + 1) * LANES] = blk

    def lane_block(ref, rows, h):
        return ref[rows, (h // 2) * LANES:(h // 2 + 1) * LANES]

    is_lo = lax.broadcasted_iota(jnp.int32, (1, LANES), 1) < HEAD_DIM
    q_heads = []
    for h in heads:
        q = lane_block(q_ref, slice(None), h)
        zero = jnp.zeros_like(q)
        q_heads.append(jnp.where(is_lo, q, zero) if h % 2 == 0 else jnp.where(is_lo, zero, q))
    c_here = c_ref[qi]
    c_end = [c_here[h:h + 1, tq - 1:tq] for h in heads]
    row_id = lax.broadcasted_iota(jnp.int32, (tq, tq), 0)
    col_id = lax.broadcasted_iota(jnp.int32, (tq, tq), 1)
    ones = jnp.ones((tq, LANES), BF16)

    def step(kj, carry, masked):
        rows = slice(kj * tq, (kj + 1) * tq)
        cb = c_ref[kj]
        scores = []
        for h in heads:
            s = _dot_nt(q_heads[h], lane_block(k_ref, rows, h)) + (c_end[h] - cb[h:h + 1, :])
            if masked:
                s = jnp.where(row_id >= col_id, s, NEG_INF)
            scores.append(s)
        weights, alphas, maxes = [], [], []
        for h in heads:
            m = carry[h][0]
            m_new = jnp.maximum(m, jnp.max(scores[h], axis=-1, keepdims=True))
            weights.append(jnp.exp2(scores[h] - m_new).astype(BF16))
            alphas.append(jnp.exp2(m - m_new))
            maxes.append(m_new)
        vvs = [jnp.concatenate([lane_block(v_ref, rows, h), ones], axis=1) for h in heads[::2]]
        return tuple((maxes[h], alphas[h] * carry[h][1] + _dot(weights[h], vvs[h // 2]))
                     for h in heads)

    init_one = (jnp.full((tq, 1), NEG_INF, F32), jnp.zeros((tq, 2 * LANES), F32))

    for n_before in range(nq):
        @pl.when(qi == n_before)
        def _(n_before=n_before):
            carry = (init_one,) * A_HEADS
            for kj in range(n_before):
                carry = step(kj, carry, False)
            acc = [a for _, a in step(n_before, carry, True)]
            out = [a[:, :LANES] * (1.0 / a[:, LANES:]) for a in acc]
            for p in range(A_HEADS // 2):
                o_ref[:, p * LANES:(p + 1) * LANES] = jnp.where(
                    is_lo, out[2 * p], out[2 * p + 1]).astype(o_ref.dtype)


def _fox_attention(aqkv, logf, *, batch, seq, tq):
    tq = min(tq, seq)
    nq = seq // tq
    qkv = aqkv.reshape(batch, seq, 3 * A_WIDTH)
    return pl.pallas_call(
        functools.partial(_fox_body, tq=tq, nq=nq),
        grid=(batch, nq),
        in_specs=[pl.BlockSpec((None, tq, A_WIDTH), lambda b, i: (b, i, 0)),
                  pl.BlockSpec((None, seq, A_WIDTH), lambda b, i: (b, 0, 1)),
                  pl.BlockSpec((None, seq, A_WIDTH), lambda b, i: (b, 0, 2)),
                  pl.BlockSpec((None, SUBLANES, seq), lambda b, i: (b, 0, 0))],
        out_specs=pl.BlockSpec((None, tq, A_WIDTH), lambda b, i: (b, i, 0)),
        out_shape=jax.ShapeDtypeStruct((batch, seq, A_WIDTH), BF16),
        scratch_shapes=[pltpu.VMEM((nq, SUBLANES, tq), F32)],
        compiler_params=_params(2),
        name="fox_attention",
    )(qkv, qkv, qkv, logf)


def _t5_causal_bucket(dist):
    max_exact = REL_BUCKETS // 2
    large = max_exact + (jnp.log(jnp.maximum(dist, 1).astype(F32) / max_exact)
                         / math.log(REL_MAX_DIST / max_exact)
                         * (REL_BUCKETS - max_exact)).astype(jnp.int32)
    large = jnp.minimum(large, REL_BUCKETS - 1)
    return jnp.where(dist < max_exact, dist, large)


def _band_bias_body(bucket_ref, rel_ref, o_ref):
    for part in range(2):
        bucket = bucket_ref[part]
        for h in range(D_Q_HEADS):
            acc = jnp.full(bucket.shape, NEG_INF, F32)
            for b in range(REL_BUCKETS):
                acc = jnp.where(bucket == b, rel_ref[b, h] * LOG2E, acc)
            o_ref[part, h] = acc


def _band_bias(rel_bias):
    dist = jnp.maximum(jnp.arange(Q_BLOCK)[:, None] + Q_BLOCK - jnp.arange(2 * Q_BLOCK)[None, :], 0)
    bucket = _t5_causal_bucket(dist).astype(jnp.int32)
    own = jnp.arange(Q_BLOCK)[None, :] <= jnp.arange(Q_BLOCK)[:, None]
    bucket = jnp.stack([jnp.where(own, bucket[:, Q_BLOCK:], -1),
                        jnp.where(own, -1, bucket[:, :Q_BLOCK])])
    bias = pl.pallas_call(
        _band_bias_body,
        in_specs=[pl.BlockSpec(memory_space=pltpu.VMEM), pl.BlockSpec(memory_space=pltpu.SMEM)],
        out_specs=pl.BlockSpec(memory_space=pltpu.VMEM),
        out_shape=jax.ShapeDtypeStruct((2, D_Q_HEADS, Q_BLOCK, Q_BLOCK), F32),
        name="band_bias",
    )(bucket, rel_bias)
    own_all = jnp.tile(own, (D_Q_HEADS, 1)).astype(BF16)
    return bias.reshape(2, D_Q_HEADS * Q_BLOCK, Q_BLOCK), own_all


def _swa_blocks(q_ref, k_prev, v_prev, kc_ref, vc_ref, bias_ref, own_ref, sink_ref, o_ref,
                *, base, nblk, no_prev):
    n_lane_blk = D_Q_HEADS // 2
    is_lo = lax.broadcasted_iota(jnp.int32, (1, LANES), 1) < HEAD_DIM
    sink = jnp.concatenate([jnp.broadcast_to(sink_ref[h:h + 1, :], (Q_BLOCK, LANES))
                            for h in range(D_Q_HEADS)], axis=0)
    ones = jnp.ones((Q_BLOCK, LANES), BF16)

    def block_rows(r):
        return slice((base + r) * Q_BLOCK, (base + r + 1) * Q_BLOCK)

    scores = []
    for r in range(nblk):
        rows = block_rows(r)
        q_blks = [q_ref[rows, j * LANES:(j + 1) * LANES] for j in range(n_lane_blk)]
        zero = jnp.zeros_like(q_blks[0])
        qq = jnp.concatenate([jnp.where(is_lo, q, zero) for q in q_blks]
                             + [jnp.where(is_lo, zero, q) for q in q_blks], axis=0)
        kp = k_prev if r == 0 else kc_ref[block_rows(r - 1), :]
        scores.append(_dot_nt(qq, jnp.concatenate([kc_ref[rows, :], kp], axis=0)))
    weights, maxes = [], []
    for r in range(nblk):
        s2 = scores[r]
        s_own = s2[:, :Q_BLOCK] + bias_ref[0]
        s_prev = s2[:, Q_BLOCK:] + bias_ref[1]
        if r == 0 and no_prev is not None:
            s_prev = s_prev + no_prev
        s = jnp.maximum(s_own, s_prev)
        m = jnp.max(jnp.maximum(s, sink), axis=-1, keepdims=True)
        e = jnp.exp2(s - m).astype(BF16)
        e_own = e * own_ref[...]
        weights.append(jnp.concatenate([e_own, e - e_own], axis=1))
        maxes.append(m)
    outs = []
    for r in range(nblk):
        rows = block_rows(r)
        vp = v_prev if r == 0 else vc_ref[block_rows(r - 1), :]
        vv = jnp.concatenate([jnp.concatenate([vc_ref[rows, :], ones], axis=1),
                              jnp.concatenate([vp, ones], axis=1)], axis=0)
        outs.append(_dot(weights[r], vv))
    half = n_lane_blk * Q_BLOCK
    for r in range(nblk):
        o2 = outs[r]
        o = o2[:, :LANES] * (1.0 / (o2[:, LANES:] + jnp.exp2(sink - maxes[r])))
        for j in range(n_lane_blk):
            lo = o[j * Q_BLOCK:(j + 1) * Q_BLOCK, :]
            hi = o[half + j * Q_BLOCK:half + (j + 1) * Q_BLOCK, :]
            o_ref[block_rows(r), j * LANES:(j + 1) * LANES] = jnp.where(is_lo, lo, hi).astype(o_ref.dtype)


def _merge_body(h_ref, ya_ref, yb_ref, yc_ref, dq_ref, kp_ref, kc_ref, vp_ref, vc_ref,
                bias_ref, own_ref, sink_ref,
                gpre_ref, wg_ref, bg_ref, wa_ref, wb_ref, wc_ref, wd_ref, wo_ref, gpost_ref,
                o_ref, yd_ref, *, tiles_per_seq):
    tm, d = h_ref.shape
    first_tile = pl.program_id(0) % tiles_per_seq == 0
    for base in range(0, tm // Q_BLOCK, SWA_GROUP):
        if base == 0:
            k_prev, v_prev = kp_ref[...], vp_ref[...]
            no_prev = jnp.where(first_tile, NEG_INF, 0.0)
        else:
            before = slice((base - 1) * Q_BLOCK, base * Q_BLOCK)
            k_prev, v_prev, no_prev = kc_ref[before, :], vc_ref[before, :], None
        _swa_blocks(dq_ref, k_prev, v_prev, kc_ref, vc_ref, bias_ref, own_ref, sink_ref, yd_ref,
                    base=base, nblk=SWA_GROUP, no_prev=no_prev)
    halves = [slice(i * tm // FFN_SPLIT, (i + 1) * tm // FFN_SPLIT) for i in range(FFN_SPLIT)]
    xs = [h_ref[r, :] for r in halves]
    us = [_rms(x, gpre_ref[...]).astype(BF16) for x in xs]
    branches = ((ya_ref, wa_ref), (yb_ref, wb_ref), (yc_ref, wc_ref), (yd_ref, wd_ref))
    merged = [jnp.zeros((tm // FFN_SPLIT, d), F32) for _ in halves]
    for i, (y_ref, w_ref) in enumerate(branches):
        for j, r in enumerate(halves):
            gate = jax.nn.sigmoid(_dot(us[j], wg_ref[:, i * d:(i + 1) * d]) + bg_ref[:, i * d:(i + 1) * d])
            merged[j] = merged[j] + gate * _dot(y_ref[r, :], w_ref[...])
    outs = [_dot(m.astype(BF16), wo_ref[...]) for m in merged]
    for r, x, out in zip(halves, xs, outs):
        o_ref[r, :] = x + _rms(out, gpost_ref[...])


def _merge(h, ya, yb, yc, dqkv, band_bias, own, sinks, layer,
           gpre, wg, bg, wa, wb, wc, wd, wo, gpost, *, seq, tm):
    n, d = h.shape
    tm = min(tm, seq)
    row = lambda t: (t, 0)
    lay = lambda *shape: _const_spec((None,) + shape, (layer,) + (0,) * len(shape))
    k_blk = D_WIDTH // LANES
    v_blk = k_blk + 1
    per_tile = tm // Q_BLOCK
    prev = lambda t: jnp.maximum(t * per_tile - 1, 0)
    return pl.pallas_call(
        functools.partial(_merge_body, tiles_per_seq=seq // tm),
        grid=(n // tm,),
        in_specs=[pl.BlockSpec((tm, d), row),
                  pl.BlockSpec((tm, A_WIDTH), row), pl.BlockSpec((tm, B_WIDTH), row),
                  pl.BlockSpec((tm, C_WIDTH), row),
                  pl.BlockSpec((tm, D_WIDTH), row),
                  pl.BlockSpec((Q_BLOCK, LANES), lambda t: (prev(t), k_blk)),
                  pl.BlockSpec((tm, LANES), lambda t: (t, k_blk)),
                  pl.BlockSpec((Q_BLOCK, LANES), lambda t: (prev(t), v_blk)),
                  pl.BlockSpec((tm, LANES), lambda t: (t, v_blk)),
                  _const_spec((2, D_Q_HEADS * Q_BLOCK, Q_BLOCK), (0, 0, 0)),
                  _const_spec((D_Q_HEADS * Q_BLOCK, Q_BLOCK), (0, 0)),
                  lay(D_Q_HEADS, LANES),
                  lay(1, d), lay(d, N_BRANCH * d), lay(1, N_BRANCH * d),
                  lay(A_WIDTH, d), lay(B_WIDTH, d), lay(C_WIDTH, d), lay(D_WIDTH, d),
                  lay(d, d), lay(1, d)],
        out_specs=pl.BlockSpec((tm, d), row),
        out_shape=jax.ShapeDtypeStruct((n, d), F32),
        scratch_shapes=[pltpu.VMEM((tm, D_WIDTH), BF16)],
        compiler_params=_params(1),
        name="merge",
    )(h, ya.reshape(n, A_WIDTH), yb, yc, dqkv, dqkv, dqkv, dqkv, dqkv, band_bias, own, sinks,
      gpre, wg, bg, wa, wb, wc, wd, wo, gpost)


def kernel(x, p, ffn1_norm_pre, ffn1_w_gu, ffn1_w_down, ffn1_norm_post, mix_norm_pre, w_in, b_forget, b_gate, conv_short, conv_dw, conv_dw_bias, conv_ln_gain, conv_ln_bias, attn_sinks, rel_bias, w_br_a, w_br_b, w_br_c, w_br_d, w_o, mix_norm_post, ffn2_norm_pre, ffn2_w_gu, ffn2_w_down, ffn2_norm_post, ple_norm_gate, w_ple_gate, w_ple, ple_norm_post):
    batch, seq, d = x.shape
    depth = w_in.shape[0]
    n = batch * seq
    bf = lambda w: w.astype(BF16)
    vec = lambda g: g.astype(F32)[:, None, :]

    scale = HEAD_DIM ** -0.5 * LOG2E
    aq = bf(w_in[..., :A_WIDTH] * scale)
    dq = (w_in[..., C_END:C_END + D_WIDTH] * scale).reshape(depth, d, D_Q_HEADS, HEAD_DIM)
    dq = bf(dq[:, :, D_HEAD_PERM, :].reshape(depth, d, D_WIDTH))
    w_in = bf(w_in)
    w_pack = jnp.concatenate(
        [aq, w_in[..., A_WIDTH:A_QKV_END], w_in[..., A_F_END:C_END], dq,
         w_in[..., C_END + D_WIDTH:D_END],
         jnp.pad(w_in[..., A_QKV_END:A_F_END], ((0, 0), (0, 0), (0, LANES - A_HEADS)))], axis=-1)
    w_gate = w_in[..., D_END:]
    b_forget_row = jnp.pad(b_forget.astype(F32), ((0, 0), (0, LANES - A_HEADS)))[:, None, :]
    w_br_d_perm = bf(w_br_d.reshape(depth, D_Q_HEADS, HEAD_DIM, d)[:, D_HEAD_PERM].reshape(depth, D_WIDTH, d))
    sinks = jnp.broadcast_to(attn_sinks.astype(F32)[:, :, None] * LOG2E, (depth, D_Q_HEADS, LANES))
    band_bias, band_own = _band_bias(rel_bias.astype(F32))

    ffn1_w_gu, ffn1_w_down = bf(ffn1_w_gu), bf(ffn1_w_down)
    ffn2_w_gu, ffn2_w_down = bf(ffn2_w_gu), bf(ffn2_w_down)
    w_br_a, w_br_b, w_br_c, w_o = bf(w_br_a), bf(w_br_b), bf(w_br_c), bf(w_o)
    w_ple_gate, w_ple = bf(w_ple_gate), bf(w_ple)
    p2 = p.reshape(depth, n, p.shape[-1])
    conv_taps = jnp.pad(conv_dw.astype(F32), ((0, 0), (0, CONV_TAP_ROWS - CONF_CONV), (0, 0)))

    h = x.reshape(n, d)
    for i in range(depth):
        h = _ffn(h, i, vec(ffn1_norm_pre), ffn1_w_gu, ffn1_w_down, vec(ffn1_norm_post), tm=FFN_TM)
        aqkv, yb, yc, dqkv, logf = _inproj(
            h, i, vec(mix_norm_pre), w_pack, b_forget_row, conv_short.astype(F32),
            conv_taps, vec(conv_dw_bias), vec(conv_ln_gain), vec(conv_ln_bias),
            batch=batch, seq=seq, tm=INPROJ_TM)
        ya = _fox_attention(aqkv, logf, batch=batch, seq=seq, tq=FOX_TQ)
        h = _merge(h, ya, yb, yc, dqkv, band_bias, band_own, sinks, i,
                   vec(mix_norm_pre), w_gate, vec(b_gate),
                   w_br_a, w_br_b, w_br_c, w_br_d_perm, w_o, vec(mix_norm_post),
                   seq=seq, tm=FFN_TM)
        h = _ffn(h, i, vec(ffn2_norm_pre), ffn2_w_gu, ffn2_w_down, vec(ffn2_norm_post),
                 ple=(p2, vec(ple_norm_gate), w_ple_gate, w_ple, vec(ple_norm_post)), tm=FFN_TM)
    return h.reshape(batch, seq, d)
```

```python
import functools
import math

import jax
import jax.numpy as jnp
from jax import lax
from jax.experimental import pallas as pl
from jax.experimental.pallas import tpu as pltpu

F32 = jnp.float32
BF16 = jnp.bfloat16

HEAD_DIM = 64
A_HEADS = 4
A_WIDTH = A_HEADS * HEAD_DIM
B_WIDTH = 256
SHORT_CONV = 3
C_WIDTH = 256
CONF_CONV = 31
D_Q_HEADS = 8
D_KV_HEADS = 2
D_WIDTH = D_Q_HEADS * HEAD_DIM
D_KV_WIDTH = D_KV_HEADS * HEAD_DIM
WINDOW = 128
Q_BLOCK = 128
N_BRANCH = 4
REL_BUCKETS = 32
REL_MAX_DIST = 128
EPS = 1e-6
NEG_INF = -1e30
LOG2E = math.log2(math.e)

LANES = 128
SUBLANES = 8
VMEM_LIMIT = 56 * 1024 * 1024
MXU_WIDTH = 256
FFN_TM = 1024
FFN_CHUNK = MXU_WIDTH
FFN_SPLIT = 4
INPROJ_TM = 1024
FOX_TQ = 512
SWA_GROUP = 8

A_QKV_END = 3 * A_WIDTH
A_F_END = A_QKV_END + A_HEADS
B_END = A_F_END + 3 * B_WIDTH
C_END = B_END + 2 * C_WIDTH
D_END = C_END + D_WIDTH + 2 * D_KV_WIDTH

D_HEAD_PERM = (0, 4, 1, 5, 2, 6, 3, 7)

P_A = 0
P_B = P_A + 3 * A_WIDTH
P_C = P_B + 3 * B_WIDTH
P_D = P_C + 2 * C_WIDTH
P_F = P_D + D_WIDTH + 2 * D_KV_WIDTH
P_END = P_F + LANES


def _rms(x, g):
    return x * lax.rsqrt(jnp.mean(x * x, axis=-1, keepdims=True) + EPS) * g


def _dot(a, b):
    return jnp.dot(a, b, preferred_element_type=F32)


def _dot_nt(a, b):
    return lax.dot_general(a, b, (((1,), (1,)), ((), ())), preferred_element_type=F32)


def _zero_after(x):
    bits = lax.bitcast_convert_type(x, jnp.uint32)
    sixteen = jnp.uint32(16)
    bits = lax.shift_right_logical(lax.shift_right_logical(bits, sixteen), sixteen)
    return lax.bitcast_convert_type(bits, F32)


def _const_spec(shape, index):
    return pl.BlockSpec(shape, lambda *_: index, pipeline_mode=pl.Buffered(1))


def _params(n_axes):
    return pltpu.CompilerParams(dimension_semantics=("arbitrary",) * n_axes,
                                vmem_limit_bytes=VMEM_LIMIT)


def _ffn_body(*refs, d_ff, n_chunks, has_ple):
    if has_ple:
        (h_ref, p_ref, gpre_ref, wgu_ref, wdn_ref, gpost_ref,
         ggate_ref, wpg_ref, wple_ref, gple_ref, o_ref, act_ref) = refs
    else:
        h_ref, gpre_ref, wgu_ref, wdn_ref, gpost_ref, o_ref, act_ref = refs
    tm = h_ref.shape[0]
    halves = [slice(i * tm // FFN_SPLIT, (i + 1) * tm // FFN_SPLIT) for i in range(FFN_SPLIT)]
    xs = [h_ref[r, :] for r in halves]
    us = [_rms(x, gpre_ref[...]).astype(BF16) for x in xs]
    fc = d_ff // n_chunks
    for c in range(n_chunks):
        for r, u in zip(halves, us):
            gate = _dot(u, wgu_ref[:, c * fc:(c + 1) * fc])
            up = _dot(u, wgu_ref[:, d_ff + c * fc:d_ff + (c + 1) * fc])
            act_ref[r, c * fc:(c + 1) * fc] = (gate * jax.nn.sigmoid(gate) * up).astype(BF16)
    fs = [_dot(act_ref[r, :], wdn_ref[...]) for r in halves]
    ys = [x + 0.5 * _rms(f, gpost_ref[...]) for x, f in zip(xs, fs)]
    if has_ple:
        pes = [_dot(p_ref[r, :].astype(BF16), wple_ref[...]) for r in halves]
        pgs = [jax.nn.sigmoid(_dot(_rms(y, ggate_ref[...]).astype(BF16), wpg_ref[...])) for y in ys]
        ys = [y + pg * _rms(pe, gple_ref[...]) for y, pg, pe in zip(ys, pgs, pes)]
    for r, y in zip(halves, ys):
        o_ref[r, :] = y


def _ffn(h, layer, gpre, wgu, wdn, gpost, ple=None, *, tm):
    n, d = h.shape
    d_ff = wdn.shape[1]
    tm = min(tm, n)
    row = lambda t: (t, 0)
    vec = _const_spec((None, 1, d), (layer, 0, 0))
    in_specs = [pl.BlockSpec((tm, d), row)]
    args = [h]
    if ple is not None:
        p, ggate, wpg, wple, gple = ple
        in_specs.append(pl.BlockSpec((None, tm, p.shape[-1]), lambda t: (layer, t, 0)))
        args.append(p)
    in_specs += [vec, _const_spec((None, d, 2 * d_ff), (layer, 0, 0)),
                 _const_spec((None, d_ff, d), (layer, 0, 0)), vec]
    args += [gpre, wgu, wdn, gpost]
    if ple is not None:
        in_specs += [vec, _const_spec((None, d, d), (layer, 0, 0)),
                     _const_spec((None, p.shape[-1], d), (layer, 0, 0)), vec]
        args += [ggate, wpg, wple, gple]
    return pl.pallas_call(
        functools.partial(_ffn_body, d_ff=d_ff, n_chunks=d_ff // FFN_CHUNK, has_ple=ple is not None),
        grid=(n // tm,),
        in_specs=in_specs,
        out_specs=pl.BlockSpec((tm, d), row),
        out_shape=jax.ShapeDtypeStruct((n, d), F32),
        scratch_shapes=[pltpu.VMEM((tm, d_ff), BF16)],
        compiler_params=_params(1),
        name="ffn_ple" if ple is not None else "ffn",
    )(*args)


CONV_HALO = 32
SHORT_HALO = SUBLANES
CONV_ROWS = 128
CONV_SPLIT = 1
CONV_TAP_ROWS = 32


def _inproj_body(h_ref, g_ref, w_ref, bf_ref, cs_ref, cdw_ref, cdb_ref, lng_ref, lnb_ref,
                 aqkv_ref, yb_ref, yc_ref, dqkv_ref, logf_ref,
                 u_ref, zext_ref, gext_ref, gsh_ref, taps_ref):
    tm = h_ref.shape[0]

    @pl.when(pl.program_id(1) == 0)
    def _():
        zext_ref[:SHORT_HALO, :] = jnp.zeros((SHORT_HALO, B_WIDTH), F32)
        gext_ref[:CONV_HALO, :] = jnp.zeros((CONV_HALO, C_WIDTH), F32)

    u_ref[...] = _rms(h_ref[...], g_ref[...]).astype(BF16)

    c = _dot(u_ref[...], w_ref[:, P_C:P_D])
    gext_ref[CONV_HALO:, :] = c[:, :C_WIDTH] * jax.nn.sigmoid(c[:, C_WIDTH:])
    sh_rows = tm + CONV_HALO - SUBLANES
    for s in range(1, SUBLANES):
        gsh_ref[s - 1] = gext_ref[s:s + sh_rows, :]

    def conv_chunk(r0, gate):
        sub = CONV_ROWS // CONV_SPLIT
        for j in range(CONV_SPLIT):
            g_row = (j + 1) * tm // CONV_SPLIT
            zero = _zero_after(gate[g_row - SUBLANES:g_row, :LANES])
            taps_ref[...] = cdw_ref[...] + jnp.tile(zero, (CONV_TAP_ROWS // SUBLANES, C_WIDTH // LANES))
            base = r0 + j * sub
            acc = jnp.broadcast_to(cdb_ref[...], (sub, C_WIDTH))
            for k in range(CONF_CONV):
                off = CONV_HALO - (CONF_CONV - 1) + k
                s = off % SUBLANES
                lo = base + off - s
                win = gext_ref[lo:lo + sub, :] if s == 0 else gsh_ref[s - 1, lo:lo + sub, :]
                acc = acc + taps_ref[k:k + 1, :] * win
            mu = jnp.mean(acc, axis=-1, keepdims=True)
            xc = acc - mu
            y = xc * lax.rsqrt(jnp.mean(xc * xc, axis=-1, keepdims=True) + EPS)
            y = y * lng_ref[...] + lnb_ref[...]
            yc_ref[base:base + sub, :] = (y * jax.nn.sigmoid(y)).astype(BF16)

    def slab(lo):
        return _dot(u_ref[...], w_ref[:, lo:lo + MXU_WIDTH])

    def proj_a(part):
        a = slab(P_A + part * A_WIDTH)
        aqkv_ref[:, part * A_WIDTH:(part + 1) * A_WIDTH] = a.astype(BF16)
        return a

    held = {}

    def proj_b(part):
        b = slab(P_B + part * B_WIDTH)
        if part < 2:
            held[part] = b
            return b
        zext_ref[SHORT_HALO:, :] = held[1] * b
        conv = jnp.zeros((tm, B_WIDTH), F32)
        for k in range(SHORT_CONV):
            off = SHORT_HALO - (SHORT_CONV - 1) + k
            conv = conv + cs_ref[k:k + 1, :] * zext_ref[off:off + tm, :]
        yb_ref[...] = (held[0] * conv).astype(BF16)
        zext_ref[:SHORT_HALO, :] = zext_ref[tm:tm + SHORT_HALO, :]
        return b

    def proj_d(part):
        dd = slab(P_D + part * MXU_WIDTH)
        dqkv_ref[:, part * MXU_WIDTH:(part + 1) * MXU_WIDTH] = dd.astype(BF16)
        return dd

    def proj_f():
        af = _dot(u_ref[...], w_ref[:, P_F:P_END]) + bf_ref[...]
        t = af.T[:SUBLANES, :]
        logf_ref[...] = (jnp.minimum(t, 0.0) - jnp.log1p(jnp.exp(-jnp.abs(t)))) * LOG2E
        return af

    n_d = (D_WIDTH + 2 * D_KV_WIDTH) // MXU_WIDTH
    projections = ([functools.partial(proj_a, i) for i in range(3)]
                   + [functools.partial(proj_b, i) for i in range(3)]
                   + [functools.partial(proj_d, i) for i in range(n_d)] + [proj_f])
    chunks = list(range(0, tm, CONV_ROWS))
    assert len(chunks) <= len(projections)
    for i, proj in enumerate(projections):
        res = proj()
        if i < len(chunks):
            conv_chunk(chunks[i], res)
    gext_ref[:CONV_HALO, :] = gext_ref[tm:tm + CONV_HALO, :]


def _inproj(h, layer, g, w, bf, cs, cdw, cdb, lng, lnb, *, batch, seq, tm):
    n, d = h.shape
    tm = min(tm, seq)
    nt = seq // tm
    row = lambda b, t: (b * nt + t, 0)
    lay = lambda *shape: _const_spec((None,) + shape, (layer,) + (0,) * len(shape))
    out_shape = (jax.ShapeDtypeStruct((n, 3 * A_WIDTH), BF16),
                 jax.ShapeDtypeStruct((n, B_WIDTH), BF16),
                 jax.ShapeDtypeStruct((n, C_WIDTH), BF16),
                 jax.ShapeDtypeStruct((n, D_WIDTH + 2 * D_KV_WIDTH), BF16),
                 jax.ShapeDtypeStruct((batch, SUBLANES, seq), F32))
    out_specs = (pl.BlockSpec((tm, 3 * A_WIDTH), row),
                 pl.BlockSpec((tm, B_WIDTH), row),
                 pl.BlockSpec((tm, C_WIDTH), row),
                 pl.BlockSpec((tm, D_WIDTH + 2 * D_KV_WIDTH), row),
                 pl.BlockSpec((None, SUBLANES, tm), lambda b, t: (b, 0, t)))
    return pl.pallas_call(
        _inproj_body,
        grid=(batch, nt),
        in_specs=[pl.BlockSpec((tm, d), row), lay(1, d), lay(d, P_END), lay(1, LANES),
                  lay(SHORT_CONV, B_WIDTH), lay(CONV_TAP_ROWS, C_WIDTH), lay(1, C_WIDTH),
                  lay(1, C_WIDTH), lay(1, C_WIDTH)],
        out_specs=out_specs,
        out_shape=out_shape,
        scratch_shapes=[pltpu.VMEM((tm, d), BF16),
                        pltpu.VMEM((tm + SHORT_HALO, B_WIDTH), F32),
                        pltpu.VMEM((tm + CONV_HALO, C_WIDTH), F32),
                        pltpu.VMEM((SUBLANES - 1, tm + CONV_HALO - SUBLANES, C_WIDTH), F32),
                        pltpu.VMEM((CONV_TAP_ROWS, C_WIDTH), F32)],
        compiler_params=_params(2),
        name="inproj",
    )(h, g, w, bf, cs, cdw, cdb, lng, lnb)


def _split3(x):
    hi = x.astype(BF16)
    r = x - hi.astype(F32)
    mid = r.astype(BF16)
    lo = (r - mid.astype(F32)).astype(BF16)
    return hi, mid, lo


def _cumsum_lanes(x):
    rows, s = x.shape
    nb = s // LANES
    stacked = jnp.concatenate([x[:, b * LANES:(b + 1) * LANES] for b in range(nb)], axis=0)
    r = lax.broadcasted_iota(jnp.int32, (LANES, LANES), 0)
    c = lax.broadcasted_iota(jnp.int32, (LANES, LANES), 1)
    upper = jnp.where(r <= c, 1.0, 0.0).astype(BF16)
    hi, mid, lo = _split3(stacked)
    within = (_dot(lo, upper) + _dot(mid, upper)) + _dot(hi, upper)
    tot = jnp.broadcast_to(within[:, LANES - 1:], within.shape)
    n = rows * nb
    rr = lax.broadcasted_iota(jnp.int32, (n, n), 0)
    cc = lax.broadcasted_iota(jnp.int32, (n, n), 1)
    shift = rows.bit_length() - 1
    same_row = (rr & (rows - 1)) == (cc & (rows - 1))
    earlier = lax.shift_right_logical(cc, shift) < lax.shift_right_logical(rr, shift)
    prev = jnp.where(same_row & earlier, 1.0, 0.0).astype(BF16)
    hi, mid, lo = _split3(tot)
    total = within + ((_dot(prev, lo) + _dot(prev, mid)) + _dot(prev, hi))
    return [total[b * rows:(b + 1) * rows, :] for b in range(nb)]


def _fox_body(q_ref, k_ref, v_ref, logf_ref, o_ref, c_ref, *, tq, nq):
    qi = pl.program_id(1)
    per_blk = tq // LANES
    heads = range(A_HEADS)

    @pl.when(qi == 0)
    def _():
        for b, blk in enumerate(_cumsum_lanes(logf_ref[...])):
            c_ref[b // per_blk, :, (b % per_blk) * LANES:(b % per_blk + 1) * LANES] = blk

    def lane_block(ref, rows, h):
        return ref[rows, (h // 2) * LANES:(h // 2 + 1) * LANES]

    is_lo = lax.broadcasted_iota(jnp.int32, (1, LANES), 1) < HEAD_DIM
    q_heads = []
    for h in heads:
        q = lane_block(q_ref, slice(None), h)
        zero = jnp.zeros_like(q)
        q_heads.append(jnp.where(is_lo, q, zero) if h % 2 == 0 else jnp.where(is_lo, zero, q))
    c_here = c_ref[qi]
    c_end = [c_here[h:h + 1, tq - 1:tq] for h in heads]
    row_id = lax.broadcasted_iota(jnp.int32, (tq, tq), 0)
    col_id = lax.broadcasted_iota(jnp.int32, (tq, tq), 1)
    ones = jnp.ones((tq, LANES), BF16)

    def step(kj, carry, masked):
        rows = slice(kj * tq, (kj + 1) * tq)
        cb = c_ref[kj]
        vvs = [jnp.concatenate([lane_block(v_ref, rows, h), ones], axis=1) for h in heads[::2]]
        out = []
        for h in heads:
            s = _dot_nt(q_heads[h], lane_block(k_ref, rows, h)) + (c_end[h] - cb[h:h + 1, :])
            if masked:
                s = jnp.where(row_id >= col_id, s, NEG_INF)
            m = carry[h][0]
            m_new = jnp.maximum(m, jnp.max(s, axis=-1, keepdims=True))
            w = jnp.exp2(s - m_new).astype(BF16)
            out.append((m_new, jnp.exp2(m - m_new) * carry[h][1] + _dot(w, vvs[h // 2])))
        return tuple(out)

    init_one = (jnp.full((tq, 1), NEG_INF, F32), jnp.zeros((tq, 2 * LANES), F32))

    for n_before in range(nq):
        @pl.when(qi == n_before)
        def _(n_before=n_before):
            carry = (init_one,) * A_HEADS
            for kj in range(n_before):
                carry = step(kj, carry, False)
            acc = [a for _, a in step(n_before, carry, True)]
            out = [a[:, :LANES] * (1.0 / a[:, LANES:]) for a in acc]
            for p in range(A_HEADS // 2):
                o_ref[:, p * LANES:(p + 1) * LANES] = jnp.where(
                    is_lo, out[2 * p], out[2 * p + 1]).astype(o_ref.dtype)


def _fox_attention(aqkv, logf, *, batch, seq, tq):
    tq = min(tq, seq)
    nq = seq // tq
    qkv = aqkv.reshape(batch, seq, 3 * A_WIDTH)
    return pl.pallas_call(
        functools.partial(_fox_body, tq=tq, nq=nq),
        grid=(batch, nq),
        in_specs=[pl.BlockSpec((None, tq, A_WIDTH), lambda b, i: (b, i, 0)),
                  pl.BlockSpec((None, seq, A_WIDTH), lambda b, i: (b, 0, 1)),
                  pl.BlockSpec((None, seq, A_WIDTH), lambda b, i: (b, 0, 2)),
                  pl.BlockSpec((None, SUBLANES, seq), lambda b, i: (b, 0, 0))],
        out_specs=pl.BlockSpec((None, tq, A_WIDTH), lambda b, i: (b, i, 0)),
        out_shape=jax.ShapeDtypeStruct((batch, seq, A_WIDTH), BF16),
        scratch_shapes=[pltpu.VMEM((nq, SUBLANES, tq), F32)],
        compiler_params=_params(2),
        name="fox_attention",
    )(qkv, qkv, qkv, logf)


def _t5_causal_bucket(dist):
    max_exact = REL_BUCKETS // 2
    large = max_exact + (jnp.log(jnp.maximum(dist, 1).astype(F32) / max_exact)
                         / math.log(REL_MAX_DIST / max_exact)
                         * (REL_BUCKETS - max_exact)).astype(jnp.int32)
    large = jnp.minimum(large, REL_BUCKETS - 1)
    return jnp.where(dist < max_exact, dist, large)


def _band_bias_body(bucket_ref, rel_ref, o_ref):
    for part in range(2):
        bucket = bucket_ref[part]
        for h in range(D_Q_HEADS):
            acc = jnp.full(bucket.shape, NEG_INF, F32)
            for b in range(REL_BUCKETS):
                acc = jnp.where(bucket == b, rel_ref[b, h] * LOG2E, acc)
            o_ref[part, h] = acc


def _band_bias(rel_bias):
    dist = jnp.maximum(jnp.arange(Q_BLOCK)[:, None] + Q_BLOCK - jnp.arange(2 * Q_BLOCK)[None, :], 0)
    bucket = _t5_causal_bucket(dist).astype(jnp.int32)
    own = jnp.arange(Q_BLOCK)[None, :] <= jnp.arange(Q_BLOCK)[:, None]
    bucket = jnp.stack([jnp.where(own, bucket[:, Q_BLOCK:], -1),
                        jnp.where(own, -1, bucket[:, :Q_BLOCK])])
    bias = pl.pallas_call(
        _band_bias_body,
        in_specs=[pl.BlockSpec(memory_space=pltpu.VMEM), pl.BlockSpec(memory_space=pltpu.SMEM)],
        out_specs=pl.BlockSpec(memory_space=pltpu.VMEM),
        out_shape=jax.ShapeDtypeStruct((2, D_Q_HEADS, Q_BLOCK, Q_BLOCK), F32),
        name="band_bias",
    )(bucket, rel_bias)
    own_all = jnp.tile(own, (D_Q_HEADS, 1)).astype(BF16)
    return bias.reshape(2, D_Q_HEADS * Q_BLOCK, Q_BLOCK), own_all


def _swa_blocks(q_ref, k_prev, v_prev, kc_ref, vc_ref, bias_ref, own_ref, sink_ref, o_ref,
                *, base, nblk, no_prev):
    n_lane_blk = D_Q_HEADS // 2
    is_lo = lax.broadcasted_iota(jnp.int32, (1, LANES), 1) < HEAD_DIM
    sink = jnp.concatenate([jnp.broadcast_to(sink_ref[h:h + 1, :], (Q_BLOCK, LANES))
                            for h in range(D_Q_HEADS)], axis=0)
    ones = jnp.ones((Q_BLOCK, LANES), BF16)

    def block_rows(r):
        return slice((base + r) * Q_BLOCK, (base + r + 1) * Q_BLOCK)

    scores = []
    for r in range(nblk):
        rows = block_rows(r)
        q_blks = [q_ref[rows, j * LANES:(j + 1) * LANES] for j in range(n_lane_blk)]
        zero = jnp.zeros_like(q_blks[0])
        qq = jnp.concatenate([jnp.where(is_lo, q, zero) for q in q_blks]
                             + [jnp.where(is_lo, zero, q) for q in q_blks], axis=0)
        kp = k_prev if r == 0 else kc_ref[block_rows(r - 1), :]
        scores.append(_dot_nt(qq, jnp.concatenate([kc_ref[rows, :], kp], axis=0)))
    weights, maxes = [], []
    for r in range(nblk):
        s2 = scores[r]
        s_own = s2[:, :Q_BLOCK] + bias_ref[0]
        s_prev = s2[:, Q_BLOCK:] + bias_ref[1]
        if r == 0 and no_prev is not None:
            s_prev = s_prev + no_prev
        s = jnp.maximum(s_own, s_prev)
        m = jnp.max(jnp.maximum(s, sink), axis=-1, keepdims=True)
        e = jnp.exp2(s - m).astype(BF16)
        e_own = e * own_ref[...]
        weights.append(jnp.concatenate([e_own, e - e_own], axis=1))
        maxes.append(m)
    outs = []
    for r in range(nblk):
        rows = block_rows(r)
        vp = v_prev if r == 0 else vc_ref[block_rows(r - 1), :]
        vv = jnp.concatenate([jnp.concatenate([vc_ref[rows, :], ones], axis=1),
                              jnp.concatenate([vp, ones], axis=1)], axis=0)
        outs.append(_dot(weights[r], vv))
    half = n_lane_blk * Q_BLOCK
    for r in range(nblk):
        o2 = outs[r]
        o = o2[:, :LANES] * (1.0 / (o2[:, LANES:] + jnp.exp2(sink - maxes[r])))
        for j in range(n_lane_blk):
            lo = o[j * Q_BLOCK:(j + 1) * Q_BLOCK, :]
            hi = o[half + j * Q_BLOCK:half + (j + 1) * Q_BLOCK, :]
            o_ref[block_rows(r), j * LANES:(j + 1) * LANES] = jnp.where(is_lo, lo, hi).astype(o_ref.dtype)


def _merge_body(h_ref, ya_ref, yb_ref, yc_ref, dq_ref, kp_ref, kc_ref, vp_ref, vc_ref,
                bias_ref, own_ref, sink_ref,
                gpre_ref, wg_ref, bg_ref, wa_ref, wb_ref, wc_ref, wd_ref, wo_ref, gpost_ref,
                o_ref, yd_ref, *, tiles_per_seq):
    tm, d = h_ref.shape
    first_tile = pl.program_id(0) % tiles_per_seq == 0
    for base in range(0, tm // Q_BLOCK, SWA_GROUP):
        if base == 0:
            k_prev, v_prev = kp_ref[...], vp_ref[...]
            no_prev = jnp.where(first_tile, NEG_INF, 0.0)
        else:
            before = slice((base - 1) * Q_BLOCK, base * Q_BLOCK)
            k_prev, v_prev, no_prev = kc_ref[before, :], vc_ref[before, :], None
        _swa_blocks(dq_ref, k_prev, v_prev, kc_ref, vc_ref, bias_ref, own_ref, sink_ref, yd_ref,
                    base=base, nblk=SWA_GROUP, no_prev=no_prev)
    halves = [slice(i * tm // FFN_SPLIT, (i + 1) * tm // FFN_SPLIT) for i in range(FFN_SPLIT)]
    xs = [h_ref[r, :] for r in halves]
    us = [_rms(x, gpre_ref[...]).astype(BF16) for x in xs]
    branches = ((ya_ref, wa_ref), (yb_ref, wb_ref), (yc_ref, wc_ref), (yd_ref, wd_ref))
    merged = [jnp.zeros((tm // FFN_SPLIT, d), F32) for _ in halves]
    for i, (y_ref, w_ref) in enumerate(branches):
        for j, r in enumerate(halves):
            gate = jax.nn.sigmoid(_dot(us[j], wg_ref[:, i * d:(i + 1) * d]) + bg_ref[:, i * d:(i + 1) * d])
            merged[j] = merged[j] + gate * _dot(y_ref[r, :], w_ref[...])
    outs = [_dot(m.astype(BF16), wo_ref[...]) for m in merged]
    for r, x, out in zip(halves, xs, outs):
        o_ref[r, :] = x + _rms(out, gpost_ref[...])


def _merge(h, ya, yb, yc, dqkv, band_bias, own, sinks, layer,
           gpre, wg, bg, wa, wb, wc, wd, wo, gpost, *, seq, tm):
    n, d = h.shape
    tm = min(tm, seq)
    row = lambda t: (t, 0)
    lay = lambda *shape: _const_spec((None,) + shape, (layer,) + (0,) * len(shape))
    k_blk = D_WIDTH // LANES
    v_blk = k_blk + 1
    per_tile = tm // Q_BLOCK
    prev = lambda t: jnp.maximum(t * per_tile - 1, 0)
    return pl.pallas_call(
        functools.partial(_merge_body, tiles_per_seq=seq // tm),
        grid=(n // tm,),
        in_specs=[pl.BlockSpec((tm, d), row),
                  pl.BlockSpec((tm, A_WIDTH), row), pl.BlockSpec((tm, B_WIDTH), row),
                  pl.BlockSpec((tm, C_WIDTH), row),
                  pl.BlockSpec((tm, D_WIDTH), row),
                  pl.BlockSpec((Q_BLOCK, LANES), lambda t: (prev(t), k_blk)),
                  pl.BlockSpec((tm, LANES), lambda t: (t, k_blk)),
                  pl.BlockSpec((Q_BLOCK, LANES), lambda t: (prev(t), v_blk)),
                  pl.BlockSpec((tm, LANES), lambda t: (t, v_blk)),
                  _const_spec((2, D_Q_HEADS * Q_BLOCK, Q_BLOCK), (0, 0, 0)),
                  _const_spec((D_Q_HEADS * Q_BLOCK, Q_BLOCK), (0, 0)),
                  lay(D_Q_HEADS, LANES),
                  lay(1, d), lay(d, N_BRANCH * d), lay(1, N_BRANCH * d),
                  lay(A_WIDTH, d), lay(B_WIDTH, d), lay(C_WIDTH, d), lay(D_WIDTH, d),
                  lay(d, d), lay(1, d)],
        out_specs=pl.BlockSpec((tm, d), row),
        out_shape=jax.ShapeDtypeStruct((n, d), F32),
        scratch_shapes=[pltpu.VMEM((tm, D_WIDTH), BF16)],
        compiler_params=_params(1),
        name="merge",
    )(h, ya.reshape(n, A_WIDTH), yb, yc, dqkv, dqkv, dqkv, dqkv, dqkv, band_bias, own, sinks,
      gpre, wg, bg, wa, wb, wc, wd, wo, gpost)


def kernel(x, p, ffn1_norm_pre, ffn1_w_gu, ffn1_w_down, ffn1_norm_post, mix_norm_pre, w_in, b_forget, b_gate, conv_short, conv_dw, conv_dw_bias, conv_ln_gain, conv_ln_bias, attn_sinks, rel_bias, w_br_a, w_br_b, w_br_c, w_br_d, w_o, mix_norm_post, ffn2_norm_pre, ffn2_w_gu, ffn2_w_down, ffn2_norm_post, ple_norm_gate, w_ple_gate, w_ple, ple_norm_post):
    batch, seq, d = x.shape
    depth = w_in.shape[0]
    n = batch * seq
    bf = lambda w: w.astype(BF16)
    vec = lambda g: g.astype(F32)[:, None, :]

    scale = HEAD_DIM ** -0.5 * LOG2E
    aq = bf(w_in[..., :A_WIDTH] * scale)
    dq = (w_in[..., C_END:C_END + D_WIDTH] * scale).reshape(depth, d, D_Q_HEADS, HEAD_DIM)
    dq = bf(dq[:, :, D_HEAD_PERM, :].reshape(depth, d, D_WIDTH))
    w_in = bf(w_in)
    w_pack = jnp.concatenate(
        [aq, w_in[..., A_WIDTH:A_QKV_END], w_in[..., A_F_END:C_END], dq,
         w_in[..., C_END + D_WIDTH:D_END],
         jnp.pad(w_in[..., A_QKV_END:A_F_END], ((0, 0), (0, 0), (0, LANES - A_HEADS)))], axis=-1)
    w_gate = w_in[..., D_END:]
    b_forget_row = jnp.pad(b_forget.astype(F32), ((0, 0), (0, LANES - A_HEADS)))[:, None, :]
    w_br_d_perm = bf(w_br_d.reshape(depth, D_Q_HEADS, HEAD_DIM, d)[:, D_HEAD_PERM].reshape(depth, D_WIDTH, d))
    sinks = jnp.broadcast_to(attn_sinks.astype(F32)[:, :, None] * LOG2E, (depth, D_Q_HEADS, LANES))
    band_bias, band_own = _band_bias(rel_bias.astype(F32))

    ffn1_w_gu, ffn1_w_down = bf(ffn1_w_gu), bf(ffn1_w_down)
    ffn2_w_gu, ffn2_w_down = bf(ffn2_w_gu), bf(ffn2_w_down)
    w_br_a, w_br_b, w_br_c, w_o = bf(w_br_a), bf(w_br_b), bf(w_br_c), bf(w_o)
    w_ple_gate, w_ple = bf(w_ple_gate), bf(w_ple)
    p2 = p.reshape(depth, n, p.shape[-1])
    conv_taps = jnp.pad(conv_dw.astype(F32), ((0, 0), (0, CONV_TAP_ROWS - CONF_CONV), (0, 0)))

    h = x.reshape(n, d)
    for i in range(depth):
        h = _ffn(h, i, vec(ffn1_norm_pre), ffn1_w_gu, ffn1_w_down, vec(ffn1_norm_post), tm=FFN_TM)
        aqkv, yb, yc, dqkv, logf = _inproj(
            h, i, vec(mix_norm_pre), w_pack, b_forget_row, conv_short.astype(F32),
            conv_taps, vec(conv_dw_bias), vec(conv_ln_gain), vec(conv_ln_bias),
            batch=batch, seq=seq, tm=INPROJ_TM)
        ya = _fox_attention(aqkv, logf, batch=batch, seq=seq, tq=FOX_TQ)
        h = _merge(h, ya, yb, yc, dqkv, band_bias, band_own, sinks, i,
                   vec(mix_norm_pre), w_gate, vec(b_gate),
                   w_br_a, w_br_b, w_br_c, w_br_d_perm, w_o, vec(mix_norm_post),
                   seq=seq, tm=FFN_TM)
        h = _ffn(h, i, vec(ffn2_norm_pre), ffn2_w_gu, ffn2_w_down, vec(ffn2_norm_post),
                 ple=(p2, vec(ple_norm_gate), w_ple_gate, w_ple, vec(ple_norm_post)), tm=FFN_TM)
    return h.reshape(batch, seq, d)
```
